```python
import jax, jax.numpy as jnp
from jax import lax
import numpy as np

D_MODEL = 2048
BATCH = 1
SEQ = 16384
DEPTH = 1

D_MIX = D_MODEL
D_GMLP = D_MIX // 2
GMLP_GROUPS = 8
GMLP_GROUP_DIM = D_GMLP // GMLP_GROUPS
GMLP_CHUNK = 128
D_MLSTM = D_MIX - D_GMLP
MLSTM_HEADS = 4
MLSTM_HEAD_DIM = D_MLSTM // MLSTM_HEADS
MLSTM_CHUNK = 128
CONV_WIDTH = 4
PEER_HEADS = 8
PEER_N_KEYS = 128
PEER_N_EXPERTS = PEER_N_KEYS * PEER_N_KEYS
PEER_QUERY_DIM = 256
PEER_HALF_DIM = PEER_QUERY_DIM // 2
PEER_TOPK = 16
PEER_TOKEN_BLOCK = 128
N_MOD = 6
EPS = 1e-6
IN_COLS = 2 * D_GMLP + 4 * D_MLSTM + 2 * MLSTM_HEADS
IN_SPLITS = (D_GMLP, 2 * D_GMLP, 2 * D_GMLP + 2 * D_MLSTM,
             2 * D_GMLP + 3 * D_MLSTM, 2 * D_GMLP + 4 * D_MLSTM)

kernel_name = "hybrid_gmlp_mlstm_peer_block"


def rms_norm(x, gain):
    xf = x.astype(jnp.float32)
    y = xf * lax.rsqrt(jnp.mean(xf * xf, axis=-1, keepdims=True) + EPS)
    return (y * gain.astype(jnp.float32)).astype(x.dtype)


def causal_depthwise_conv(x, w, b):
    y = lax.conv_general_dilated(
        x, w[:, None, :].astype(x.dtype), window_strides=(1,),
        padding=[(CONV_WIDTH - 1, 0)], dimension_numbers=("NWC", "WIO", "NWC"),
        feature_group_count=x.shape[-1])
    return y + b


def gmlp_spatial_gating(u, v, norm_g, w_spatial, b_spatial):
    B, S, _ = u.shape
    nc = S // GMLP_CHUNK
    u = u.reshape(B, nc, GMLP_CHUNK, GMLP_GROUPS, GMLP_GROUP_DIM)
    v = rms_norm(v.reshape(B, nc, GMLP_CHUNK, GMLP_GROUPS, GMLP_GROUP_DIM), norm_g)
    pos = jnp.arange(GMLP_CHUNK)
    causal = pos[:, None] >= pos[None, :]
    w = jnp.where(causal, w_spatial, 0)
    mixed = jnp.einsum("gts,bnsgc->bntgc", w, v) + b_spatial.T[None, None, :, :, None]
    return (u * mixed).reshape(B, S, D_GMLP)


def mlstm_chunkwise(q, k, v, i_pre, log_f):
    out_dtype = v.dtype
    q, k, v = (t.astype(jnp.float32) for t in (q, k, v))
    i_pre, log_f = i_pre.astype(jnp.float32), log_f.astype(jnp.float32)
    B, H, S, d = q.shape
    L = MLSTM_CHUNK
    nc = S // L
    q = q * (d ** -0.5)

    def to_chunks(t):
        return jnp.moveaxis(t.reshape(B, H, nc, L, *t.shape[3:]), 2, 0)

    xs = tuple(to_chunks(t) for t in (q, k, v, i_pre, log_f))
    pos = jnp.arange(L)
    causal = pos[:, None] >= pos[None, :]

    def step(carry, inp):
        C, n, m = carry
        qb, kb, vb, ib, fb = inp
        b = jnp.cumsum(fb, axis=-1)
        log_d = jnp.where(causal, b[..., :, None] - b[..., None, :] + ib[..., None, :], -jnp.inf)
        a = b + m[..., None]
        m_comb = jnp.maximum(a, jnp.max(log_d, axis=-1))
        w_intra = jnp.exp(log_d - m_comb[..., None])
        w_inter = jnp.exp(a - m_comb)
        s = jnp.einsum("bhtd,bhsd->bhts", qb, kb) * w_intra
        num = (jnp.einsum("bhts,bhsd->bhtd", s, vb)
               + w_inter[..., None] * jnp.einsum("bhtd,bhde->bhte", qb, C))
        den = jnp.sum(s, axis=-1) + w_inter * jnp.einsum("bhtd,bhd->bht", qb, n)
        h = num / jnp.maximum(jnp.abs(den), jnp.exp(-m_comb))[..., None]
        b_last = b[..., -1]
        log_w = b_last[..., None] - b + ib
        m_new = jnp.maximum(b_last + m, jnp.max(log_w, axis=-1))
        w_state = jnp.exp(log_w - m_new[..., None])
        decay = jnp.exp(b_last + m - m_new)
        C_new = decay[..., None, None] * C + jnp.einsum("bhs,bhsd,bhse->bhde", w_state, kb, vb)
        n_new = decay[..., None] * n + jnp.einsum("bhs,bhsd->bhd", w_state, kb)
        return (C_new, n_new, m_new), h

    init = (jnp.zeros((B, H, d, d), jnp.float32), jnp.zeros((B, H, d), jnp.float32),
            jnp.zeros((B, H), jnp.float32))
    _, h = lax.scan(step, init, xs)
    return jnp.moveaxis(h, 0, 2).reshape(B, H, S, d).astype(out_dtype)


def peer_ffn(h, w_query, sub_keys_1, sub_keys_2, expert_down, expert_up):
    B, S, D = h.shape
    T = B * S
    x = h.reshape(T, D)
    q = (x @ w_query).reshape(T, PEER_HEADS, 2, PEER_HALF_DIM)
    s1 = jnp.einsum("thc,kc->thk", q[:, :, 0], sub_keys_1)
    s2 = jnp.einsum("thc,kc->thk", q[:, :, 1], sub_keys_2)
    v1, i1 = lax.top_k(s1, PEER_TOPK)
    v2, i2 = lax.top_k(s2, PEER_TOPK)
    cand_s = (v1[..., :, None] + v2[..., None, :]).reshape(T, PEER_HEADS, PEER_TOPK * PEER_TOPK)
    cand_i = (i1[..., :, None] * PEER_N_KEYS + i2[..., None, :]).reshape(T, PEER_HEADS, PEER_TOPK * PEER_TOPK)
    top_s, top_pos = lax.top_k(cand_s, PEER_TOPK)
    idx = jnp.take_along_axis(cand_i, top_pos, axis=-1)
    gates = jax.nn.softmax(top_s.astype(jnp.float32), axis=-1).astype(x.dtype)

    nb = T // PEER_TOKEN_BLOCK

    def expert_block(args):
        xb, ib, gb = args
        u = expert_down[ib]
        act = jax.nn.gelu(jnp.einsum("td,thkd->thk", xb, u), approximate=False)
        return jnp.einsum("thk,thkd->td", gb * act, expert_up[ib])

    y = lax.map(expert_block, (x.reshape(nb, PEER_TOKEN_BLOCK, D),
                               idx.reshape(nb, PEER_TOKEN_BLOCK, PEER_HEADS, PEER_TOPK),
                               gates.reshape(nb, PEER_TOKEN_BLOCK, PEER_HEADS, PEER_TOPK)))
    return y.reshape(B, S, D)


def setup_inputs(seed: int = 0) -> dict:
    key = jax.random.key(seed)
    ks = jax.random.split(key, 24)
    nrm = jax.random.normal
    D, H, dh = D_MODEL, MLSTM_HEADS, MLSTM_HEAD_DIM
    b_gates = jnp.concatenate(
        [0.1 * nrm(ks[6], (DEPTH, H)),
         jnp.linspace(3.0, 6.0, H)[None, :] + 0.1 * nrm(ks[7], (DEPTH, H))], axis=-1)
    return {
        "x": nrm(ks[0], (BATCH, SEQ, D)),
        "c": nrm(ks[1], (BATCH, D)),
        "ada_w": nrm(ks[2], (DEPTH, D, N_MOD * D)) * (0.5 * D ** -0.5),
        "ada_b": 0.02 * nrm(ks[3], (DEPTH, N_MOD * D)),
        "norm_mix_g": 1.0 + 0.02 * nrm(ks[4], (DEPTH, D)),
        "w_in": nrm(ks[5], (DEPTH, D, IN_COLS)) * D ** -0.5,
        "b_gates": b_gates,
        "conv_w": nrm(ks[8], (DEPTH, CONV_WIDTH, 2 * D_MLSTM)) * CONV_WIDTH ** -0.5,
        "conv_b": 0.02 * nrm(ks[9], (DEPTH, 2 * D_MLSTM)),
        "gmlp_norm_g": 1.0 + 0.02 * nrm(ks[10], (DEPTH, GMLP_GROUPS, GMLP_GROUP_DIM)),
        "gmlp_w_spatial": nrm(ks[11], (DEPTH, GMLP_GROUPS, GMLP_CHUNK, GMLP_CHUNK)) * (0.5 * GMLP_CHUNK ** -0.5),
        "gmlp_b_spatial": 1.0 + 0.02 * nrm(ks[12], (DEPTH, GMLP_GROUPS, GMLP_CHUNK)),
        "mlstm_norm_g": 1.0 + 0.02 * nrm(ks[13], (DEPTH, H, dh)),
        "w_out": nrm(ks[14], (DEPTH, D_MIX, D)) * D_MIX ** -0.5,
        "norm_ffn_g": 1.0 + 0.02 * nrm(ks[15], (DEPTH, D)),
        "peer_w_query": nrm(ks[16], (DEPTH, D, PEER_HEADS * PEER_QUERY_DIM)) * D ** -0.5,
        "peer_sub_keys_1": nrm(ks[17], (DEPTH, PEER_N_KEYS, PEER_HALF_DIM)) * PEER_HALF_DIM ** -0.5,
        "peer_sub_keys_2": nrm(ks[18], (DEPTH, PEER_N_KEYS, PEER_HALF_DIM)) * PEER_HALF_DIM ** -0.5,
        "peer_expert_down": nrm(ks[19], (DEPTH, PEER_N_EXPERTS, D)) * D ** -0.5,
        "peer_expert_up": nrm(ks[20], (DEPTH, PEER_N_EXPERTS, D)) * PEER_HEADS ** -0.5,
        "final_norm_g": 1.0 + 0.02 * nrm(ks[21], (D,)),
    }


def reference(x, c, ada_w, ada_b, norm_mix_g, w_in, b_gates, conv_w, conv_b, gmlp_norm_g,
              gmlp_w_spatial, gmlp_b_spatial, mlstm_norm_g, w_out, norm_ffn_g, peer_w_query,
              peer_sub_keys_1, peer_sub_keys_2, peer_expert_down, peer_expert_up, final_norm_g):
    B, S, _ = x.shape
    H, dh = MLSTM_HEADS, MLSTM_HEAD_DIM

    def heads(t):
        return t.reshape(B, S, H, dh).transpose(0, 2, 1, 3)

    for l in range(DEPTH):
        mod = jax.nn.silu(c) @ ada_w[l] + ada_b[l]
        shift_m, scale_m, gate_m, shift_f, scale_f, gate_f = jnp.split(mod[:, None, :], N_MOD, axis=-1)

        h = rms_norm(x, norm_mix_g[l]) * (1 + scale_m) + shift_m
        proj = h @ w_in[l]
        u_g, v_g, qk_m, v_m, o_m, if_m = jnp.split(proj, IN_SPLITS, axis=-1)

        y_g = gmlp_spatial_gating(jax.nn.gelu(u_g, approximate=False),
                                  jax.nn.gelu(v_g, approximate=False),
                                  gmlp_norm_g[l], gmlp_w_spatial[l], gmlp_b_spatial[l])

        qk = jax.nn.silu(causal_depthwise_conv(qk_m, conv_w[l], conv_b[l]))
        q_m, k_m = jnp.split(qk, 2, axis=-1)
        i_pre, f_pre = jnp.split(if_m + b_gates[l], 2, axis=-1)
        log_f = jax.nn.log_sigmoid(f_pre.astype(jnp.float32))
        h_m = mlstm_chunkwise(heads(q_m), heads(k_m), heads(v_m),
                              i_pre.transpose(0, 2, 1), log_f.transpose(0, 2, 1))
        h_m = rms_norm(h_m.transpose(0, 2, 1, 3), mlstm_norm_g[l]) * jax.nn.sigmoid(o_m).reshape(B, S, H, dh)
        y_m = h_m.reshape(B, S, D_MLSTM)

        mix = jnp.concatenate([y_g, y_m], axis=-1) @ w_out[l]
        x = x + gate_m * mix

        h = rms_norm(x, norm_ffn_g[l]) * (1 + scale_f) + shift_f
        x = x + gate_f * peer_ffn(h, peer_w_query[l], peer_sub_keys_1[l], peer_sub_keys_2[l],
                                  peer_expert_down[l], peer_expert_up[l])

    return rms_norm(x, final_norm_g)
```

```python
import functools
import math

import jax
import jax.numpy as jnp
from jax import lax
from jax.experimental import pallas as pl
from jax.experimental.pallas import tpu as pltpu

D_MODEL = 2048
D_GMLP = 1024
GMLP_GROUPS = 8
GMLP_GROUP_DIM = 128
CHUNK = 128
D_MLSTM = 1024
MLSTM_HEADS = 4
MLSTM_HEAD_DIM = 256
CONV_WIDTH = 4
PEER_HEADS = 8
PEER_N_KEYS = 128
PEER_N_EXPERTS = PEER_N_KEYS * PEER_N_KEYS
PEER_HALF_DIM = 128
PEER_TOPK = 16
N_MOD = 6
EPS = 1e-6
PROJ_COLS = 2 * D_GMLP + 4 * D_MLSTM
GATE_COLS = 2 * MLSTM_HEADS
LANES = 128
SUBLANES = 8
VMEM_LIMIT = 56 * 1024 * 1024

F32 = jnp.float32
BF16 = jnp.bfloat16
NEG_INF = float("-inf")
POS_INF = float("inf")
HIGHEST = lax.Precision.HIGHEST


def _params(*sem):
    return pltpu.CompilerParams(dimension_semantics=sem, vmem_limit_bytes=VMEM_LIMIT)


def _gelu(x):
    return 0.5 * x * (1.0 + lax.erf(x * (1.0 / math.sqrt(2.0))))


def _sigmoid(x):
    return 1.0 / (1.0 + jnp.exp(-x))


def _rms(x):
    return x * lax.rsqrt(jnp.mean(x * x, axis=-1, keepdims=True) + EPS)


def _dot(a, b, **kw):
    return jnp.dot(a, b, preferred_element_type=F32, **kw)


def _dot_nt(a, b, **kw):
    return lax.dot_general(a, b, (((1,), (1,)), ((), ())), preferred_element_type=F32, **kw)


def _dot_tn(a, b, **kw):
    return lax.dot_general(a, b, (((0,), (0,)), ((), ())), preferred_element_type=F32, **kw)


def _ada_kernel(c_ref, w_ref, b_ref, o_ref):
    c = c_ref[...]
    sc = c * _sigmoid(c)
    o_ref[...] = _dot(sc, w_ref[...], precision=HIGHEST) + b_ref[...]


def _ada(c8, ada_w, ada_b):
    d, n = ada_w.shape
    tn = 1024
    return pl.pallas_call(
        _ada_kernel,
        grid=(n // tn,),
        in_specs=[pl.BlockSpec((SUBLANES, d), lambda j: (0, 0)),
                  pl.BlockSpec((d, tn), lambda j: (0, j)),
                  pl.BlockSpec((1, tn), lambda j: (0, j))],
        out_specs=pl.BlockSpec((SUBLANES, tn), lambda j: (0, j)),
        out_shape=jax.ShapeDtypeStruct((SUBLANES, n), F32),
        compiler_params=_params("arbitrary"),
        name="ada",
    )(c8, ada_w, ada_b)


def _inproj_kernel(x_ref, g_ref, sc_ref, sh_ref, w_ref, wg_ref, bg_ref, proj_ref, gate_ref, h_ref):
    @pl.when(pl.program_id(1) == 0)
    def _():
        h = _rms(x_ref[...]) * g_ref[...] * (1.0 + sc_ref[...]) + sh_ref[...]
        h_ref[...] = h.astype(BF16)
        gate_ref[...] = _dot(h, wg_ref[...], precision=HIGHEST) + bg_ref[...]

    proj_ref[...] = _dot(h_ref[...], w_ref[...])


def _inproj(x, g, scale, shift, w_bf, w_gate, b_gate, tm, tn):
    t, d = x.shape
    n = w_bf.shape[1]
    row = lambda i, j: (0, 0)
    return pl.pallas_call(
        _inproj_kernel,
        grid=(t // tm, n // tn),
        in_specs=[pl.BlockSpec((tm, d), lambda i, j: (i, 0)),
                  pl.BlockSpec((1, d), row), pl.BlockSpec((1, d), row), pl.BlockSpec((1, d), row),
                  pl.BlockSpec((d, tn), lambda i, j: (0, j)),
                  pl.BlockSpec((d, LANES), row), pl.BlockSpec((1, LANES), row)],
        out_specs=[pl.BlockSpec((tm, tn), lambda i, j: (i, j)),
                   pl.BlockSpec((tm, LANES), lambda i, j: (i, 0))],
        out_shape=[jax.ShapeDtypeStruct((t, n), F32), jax.ShapeDtypeStruct((t, LANES), F32)],
        scratch_shapes=[pltpu.VMEM((tm, d), BF16)],
        compiler_params=_params("parallel", "arbitrary"),
        name="inproj",
    )(x, g, scale, shift, w_bf, w_gate, b_gate)


def _gmlp_kernel(u_ref, v_ref, w_ref, bt_ref, ng_ref, o_ref, *, chunks):
    row = lax.broadcasted_iota(jnp.int32, (CHUNK, CHUNK), 0)
    col = lax.broadcasted_iota(jnp.int32, (CHUNK, CHUNK), 1)
    causal = row >= col
    for g in range(GMLP_GROUPS):
        w = jnp.where(causal, w_ref[g], 0.0).astype(BF16)
        bias = bt_ref[:, g:g + 1]
        ng = ng_ref[g:g + 1, :]
        cs = slice(g * GMLP_GROUP_DIM, (g + 1) * GMLP_GROUP_DIM)
        for n in range(chunks):
            rs = slice(n * CHUNK, (n + 1) * CHUNK)
            vn = _rms(_gelu(v_ref[rs, cs])) * ng
            mixed = _dot(w, vn.astype(BF16)) + bias
            o_ref[rs, cs] = (_gelu(u_ref[rs, cs]) * mixed).astype(BF16)


def _gmlp(proj, w_spatial, b_spatial_t, norm_g, tg):
    t = proj.shape[0]
    full = lambda *s: pl.BlockSpec(s, lambda i: (0,) * len(s))
    return pl.pallas_call(
        functools.partial(_gmlp_kernel, chunks=tg // CHUNK),
        grid=(t // tg,),
        in_specs=[pl.BlockSpec((tg, D_GMLP), lambda i: (i, 0)),
                  pl.BlockSpec((tg, D_GMLP), lambda i: (i, 1)),
                  full(GMLP_GROUPS, CHUNK, CHUNK), full(CHUNK, GMLP_GROUPS),
                  full(GMLP_GROUPS, GMLP_GROUP_DIM)],
        out_specs=pl.BlockSpec((tg, D_GMLP), lambda i: (i, 0)),
        out_shape=jax.ShapeDtypeStruct((t, D_GMLP), BF16),
        compiler_params=_params("parallel"),
        name="gmlp",
    )(proj, proj, w_spatial, b_spatial_t, norm_g)


def _mlstm_kernel(qk_ref, v_ref, o_ref, gate_ref, cw_ref, cb_ref, ng_ref, out_ref,
                  xext_ref, c_ref, n_ref, m_ref):
    L = CHUNK
    dh = MLSTM_HEAD_DIM

    @pl.when(pl.program_id(0) == 0)
    def _():
        xext_ref[0:SUBLANES, :] = jnp.zeros((SUBLANES, 2 * D_MLSTM), F32)
        c_ref[...] = jnp.zeros_like(c_ref)
        n_ref[...] = jnp.zeros_like(n_ref)
        m_ref[...] = jnp.zeros_like(m_ref)

    xext_ref[SUBLANES:SUBLANES + L, :] = qk_ref[...]
    conv = cb_ref[...]
    for k in range(CONV_WIDTH):
        off = SUBLANES - (CONV_WIDTH - 1) + k
        conv = conv + cw_ref[k:k + 1, :] * xext_ref[off:off + L, :]
    xext_ref[0:SUBLANES, :] = xext_ref[L:L + SUBLANES, :]
    qk = conv * _sigmoid(conv)

    gates = gate_ref[...]
    log_f = jnp.minimum(gates, 0.0) - jnp.log(1.0 + jnp.exp(-jnp.abs(gates)))
    row = lax.broadcasted_iota(jnp.int32, (L, L), 0)
    col = lax.broadcasted_iota(jnp.int32, (L, L), 1)
    causal = row >= col
    tri = jnp.where(causal, 1.0, 0.0).astype(F32)
    cum = _dot(tri, log_f, precision=HIGHEST)
    lane = lax.broadcasted_iota(jnp.int32, (L, LANES), 1)
    mat = jnp.where(lane < MLSTM_HEADS, gates, cum)
    mat_t = mat.T

    for h in range(MLSTM_HEADS):
        hs = slice(h * dh, (h + 1) * dh)
        q = (qk[:, h * dh:(h + 1) * dh] * (dh ** -0.5)).astype(BF16)
        k = qk[:, D_MLSTM + h * dh:D_MLSTM + (h + 1) * dh]
        v = v_ref[:, hs].astype(BF16)
        i_col = mat[:, h:h + 1]
        b_col = mat[:, MLSTM_HEADS + h:MLSTM_HEADS + h + 1]
        i_row = mat_t[h:h + 1, :]
        b_row = mat_t[MLSTM_HEADS + h:MLSTM_HEADS + h + 1, :]
        m_prev = m_ref[h][0:1, 0:1]
        c_prev = c_ref[h]
        n_prev = n_ref[h][0:1, :]

        log_d = jnp.where(causal, b_col - b_row + i_row, NEG_INF)
        a = b_col + m_prev
        m_comb = jnp.maximum(a, jnp.max(log_d, axis=-1, keepdims=True))
        w_intra = jnp.exp(log_d - m_comb)
        w_inter = jnp.exp(a - m_comb)
        s = _dot_nt(q, k.astype(BF16)) * w_intra
        num = _dot(s.astype(BF16), v) + w_inter * _dot(q, c_prev.astype(BF16))
        qf = q.astype(F32)
        den = (jnp.sum(s, axis=-1, keepdims=True)
               + w_inter * jnp.sum(qf * n_prev, axis=-1, keepdims=True))
        hout = num / jnp.maximum(jnp.abs(den), jnp.exp(-m_comb))

        b_last = b_col[L - 1:L, :]
        log_w = b_last - b_col + i_col
        m_new = jnp.maximum(b_last + m_prev, jnp.max(log_w, axis=0, keepdims=True))
        w_state = jnp.exp(log_w - m_new)
        decay = jnp.exp(b_last + m_prev - m_new)
        kw = w_state * k
        c_ref[h] = decay * c_prev + _dot_tn(kw.astype(BF16), v)
        n_ref[h] = jnp.broadcast_to(decay * n_prev + jnp.sum(kw, axis=0, keepdims=True),
                                    (SUBLANES, dh))
        m_ref[h] = jnp.broadcast_to(m_new, (SUBLANES, LANES))

        y = _rms(hout) * ng_ref[:, hs] * _sigmoid(o_ref[:, hs])
        out_ref[:, hs] = y.astype(BF16)


def _mlstm(proj, gates, conv_w, conv_b, norm_g):
    t = proj.shape[0]
    L = CHUNK
    full = lambda *s: pl.BlockSpec(s, lambda i: (0,) * len(s))
    return pl.pallas_call(
        _mlstm_kernel,
        grid=(t // L,),
        in_specs=[pl.BlockSpec((L, 2 * D_MLSTM), lambda i: (i, 1)),
                  pl.BlockSpec((L, D_MLSTM), lambda i: (i, 4)),
                  pl.BlockSpec((L, D_MLSTM), lambda i: (i, 5)),
                  pl.BlockSpec((L, LANES), lambda i: (i, 0)),
                  full(CONV_WIDTH, 2 * D_MLSTM), full(1, 2 * D_MLSTM), full(1, D_MLSTM)],
        out_specs=pl.BlockSpec((L, D_MLSTM), lambda i: (i, 0)),
        out_shape=jax.ShapeDtypeStruct((t, D_MLSTM), BF16),
        scratch_shapes=[pltpu.VMEM((L + SUBLANES, 2 * D_MLSTM), F32),
                        pltpu.VMEM((MLSTM_HEADS, MLSTM_HEAD_DIM, MLSTM_HEAD_DIM), F32),
                        pltpu.VMEM((MLSTM_HEADS, SUBLANES, MLSTM_HEAD_DIM), F32),
                        pltpu.VMEM((MLSTM_HEADS, SUBLANES, LANES), F32)],
        compiler_params=_params("arbitrary"),
        name="mlstm",
    )(proj, proj, proj, gates, conv_w, conv_b, norm_g)


def _outproj_kernel(yg_ref, ym_ref, x_ref, wo_ref, gm_ref, g_ref, sc_ref, sh_ref, wq_ref,
                    k1_ref, k2_ref, x1_ref, h2_ref, s1_ref, s2_ref):
    mix = _dot(yg_ref[...], wo_ref[0:D_GMLP, :]) + _dot(ym_ref[...], wo_ref[D_GMLP:, :])
    x1 = x_ref[...] + gm_ref[...] * mix
    x1_ref[...] = x1
    h2 = (_rms(x1) * g_ref[...] * (1.0 + sc_ref[...]) + sh_ref[...]).astype(BF16)
    h2_ref[...] = h2
    q = _dot(h2, wq_ref[...])
    k1 = k1_ref[...]
    k2 = k2_ref[...]
    for h in range(PEER_HEADS):
        base = h * 2 * PEER_HALF_DIM
        s1_ref[h] = _dot_nt(k1, q[:, base:base + PEER_HALF_DIM], precision=HIGHEST)
        s2_ref[h] = _dot_nt(k2, q[:, base + PEER_HALF_DIM:base + 2 * PEER_HALF_DIM],
                            precision=HIGHEST)


def _outproj(yg, ym, x, w_out, gate_m, g, scale, shift, w_query, keys1, keys2, tm):
    t, d = x.shape
    row = lambda i: (0, 0)
    tok = lambda w: pl.BlockSpec((tm, w), lambda i: (i, 0))
    score = pl.BlockSpec((PEER_HEADS, PEER_N_KEYS, tm), lambda i: (0, 0, i))
    return pl.pallas_call(
        _outproj_kernel,
        grid=(t // tm,),
        in_specs=[tok(D_GMLP), tok(D_MLSTM), tok(d),
                  pl.BlockSpec((d, d), row, pipeline_mode=pl.Buffered(1)),
                  pl.BlockSpec((1, d), row), pl.BlockSpec((1, d), row),
                  pl.BlockSpec((1, d), row), pl.BlockSpec((1, d), row),
                  pl.BlockSpec((d, d), row, pipeline_mode=pl.Buffered(1)),
                  pl.BlockSpec((PEER_N_KEYS, PEER_HALF_DIM), row),
                  pl.BlockSpec((PEER_N_KEYS, PEER_HALF_DIM), row)],
        out_specs=[tok(d), tok(d), score, score],
        out_shape=[jax.ShapeDtypeStruct((t, d), F32), jax.ShapeDtypeStruct((t, d), BF16),
                   jax.ShapeDtypeStruct((PEER_HEADS, PEER_N_KEYS, t), F32),
                   jax.ShapeDtypeStruct((PEER_HEADS, PEER_N_KEYS, t), F32)],
        compiler_params=_params("parallel"),
        name="outproj",
    )(yg, ym, x, w_out, gate_m, g, scale, shift, w_query, keys1, keys2)


def _pair_candidates():
    return [(a, b) for a in range(PEER_TOPK) for b in range(PEER_TOPK)
            if (a + 1) * (b + 1) <= PEER_TOPK]


def _top_values(s):
    out = []
    cur = s
    for _ in range(PEER_TOPK):
        m = jnp.max(cur, axis=0, keepdims=True)
        out.append(m)
        cur = jnp.where(cur == m, NEG_INF, cur)
    return out


def _route_kernel(s1_ref, s2_ref, thr_ref, e1_ref, e2_ref):
    v1 = []
    v2 = []
    for h in range(PEER_HEADS):
        v1.append(_top_values(s1_ref[h]))
        v2.append(_top_values(s2_ref[h]))
    v1s = [jnp.concatenate([v1[h][a] for h in range(PEER_HEADS)], axis=0) for a in range(PEER_TOPK)]
    v2s = [jnp.concatenate([v2[h][b] for h in range(PEER_HEADS)], axis=0) for b in range(PEER_TOPK)]
    cands = [v1s[a] + v2s[b] for a, b in _pair_candidates()]
    tau = None
    for _ in range(PEER_TOPK):
        m = cands[0]
        for cnd in cands[1:]:
            m = jnp.maximum(m, cnd)
        tau = m
        cands = [jnp.where(cnd == m, NEG_INF, cnd) for cnd in cands]

    for h in range(PEER_HEADS):
        s1 = s1_ref[h]
        s2 = s2_ref[h]
        tau_h = tau[h:h + 1, :]
        e1 = jnp.exp(s1 - v1[h][0])
        e2_ref[h] = jnp.exp(s2 - v2[h][0])
        thr = jnp.full(s1.shape, POS_INF, F32)
        zp = jnp.zeros(s1.shape, F32)
        for b in range(PEER_TOPK):
            vb = v2[h][b]
            hit = (s1 + vb) >= tau_h
            thr = jnp.where(hit, jnp.minimum(thr, vb), thr)
            zp = zp + jnp.where(hit, jnp.exp(vb - v2[h][0]), 0.0)
        z = jnp.sum(e1 * zp, axis=0, keepdims=True)
        thr_ref[h] = thr
        e1_ref[h] = e1 / z


def _route(s1t, s2t, tc):
    t = s1t.shape[2]
    spec = pl.BlockSpec((PEER_HEADS, PEER_N_KEYS, tc), lambda i: (0, 0, i))
    shape = jax.ShapeDtypeStruct(s1t.shape, F32)
    return pl.pallas_call(
        _route_kernel,
        grid=(t // tc,),
        in_specs=[spec, spec],
        out_specs=[spec, spec, spec],
        out_shape=[shape, shape, shape],
        compiler_params=_params("parallel"),
        name="route",
    )(s1t, s2t)


def _peer_kernel(h2_ref, down_ref, up_ref, s2_ref, e2_ref, thr_ref, e1_ref, x1_ref, gf_ref, fg_ref,
                 out_ref, acc_ref, a_ref, w_ref, *, tb):
    j = pl.program_id(1)

    @pl.when(j == 0)
    def _():
        acc_ref[...] = jnp.zeros_like(acc_ref)

    a_ref[...] = _dot_nt(h2_ref[...], down_ref[...])

    groups = PEER_N_KEYS // SUBLANES

    rows = pl.ds(pl.multiple_of(j * SUBLANES, SUBLANES), SUBLANES)

    def column(c, carry):
        tok = pl.ds(pl.multiple_of(c * LANES, LANES), LANES)
        thr8 = [thr_ref[h, rows, tok] for h in range(PEER_HEADS)]
        e18 = [e1_ref[h, rows, tok] for h in range(PEER_HEADS)]
        for s in range(SUBLANES):
            acc = [jnp.zeros((SUBLANES, LANES), F32) for _ in range(groups)]
            for h in range(PEER_HEADS):
                thr = jnp.broadcast_to(thr8[h][s:s + 1, :], (SUBLANES, LANES))
                e1 = jnp.broadcast_to(e18[h][s:s + 1, :], (SUBLANES, LANES))
                for g in range(groups):
                    ks = pl.ds(g * SUBLANES, SUBLANES)
                    sel = jnp.where(s2_ref[h, ks, tok] >= thr, e2_ref[h, ks, tok], 0.0)
                    acc[g] = acc[g] + sel * e1
            gt = jnp.concatenate(acc, axis=0)
            es = slice(s * PEER_N_KEYS, (s + 1) * PEER_N_KEYS)
            w_ref[tok, es] = (_gelu(a_ref[tok, es]) * gt.T).astype(BF16)
        return carry

    lax.fori_loop(0, tb // LANES, column, 0)
    acc_ref[...] += _dot(w_ref[...], up_ref[...])

    @pl.when(j == pl.num_programs(1) - 1)
    def _():
        x2 = x1_ref[...] + gf_ref[...] * acc_ref[...]
        out_ref[...] = _rms(x2) * fg_ref[...]


def _peer(h2, down_bf, up_bf, s2t, e2t, thrt, e1t, x1, gate_f, final_g, tb):
    t, d = x1.shape
    se = SUBLANES * PEER_N_KEYS
    row = lambda i, j: (0, 0)
    tok = lambda: pl.BlockSpec((tb, d), lambda i, j: (i, 0))
    slab = lambda: pl.BlockSpec((se, d), lambda i, j: (j, 0))
    route = lambda: pl.BlockSpec((PEER_HEADS, PEER_N_KEYS, tb), lambda i, j: (0, 0, i),
                                 pipeline_mode=pl.Buffered(1))
    return pl.pallas_call(
        functools.partial(_peer_kernel, tb=tb),
        grid=(t // tb, PEER_N_EXPERTS // se),
        in_specs=[tok(), slab(), slab(), route(), route(), route(), route(),
                  pl.BlockSpec((tb, d), lambda i, j: (i, 0), pipeline_mode=pl.Buffered(1)),
                  pl.BlockSpec((1, d), row), pl.BlockSpec((1, d), row)],
        out_specs=tok(),
        out_shape=jax.ShapeDtypeStruct((t, d), F32),
        scratch_shapes=[pltpu.VMEM((tb, d), F32), pltpu.VMEM((tb, se), F32),
                        pltpu.VMEM((tb, se), BF16)],
        compiler_params=_params("parallel", "arbitrary"),
        name="peer",
    )(h2, down_bf, up_bf, s2t, e2t, thrt, e1t, x1, gate_f, final_g)


def _pick(n, *cands):
    for cnd in cands:
        if n % cnd == 0:
            return cnd
    raise ValueError(f"unsupported token count {n}")


def _tiles(t):
    return {
        "inproj": dict(tm=_pick(t, 1024, 512, 256, 128), tn=1024),
        "gmlp": dict(tg=_pick(t, 512, 256, 128)),
        "mlstm": dict(),
        "outproj": dict(tm=_pick(t, 256, 128)),
        "route": dict(tc=_pick(t, 512, 256, 128)),
        "peer": dict(tb=_pick(t, 512, 256, 128)),
    }


def kernel(x, c, ada_w, ada_b, norm_mix_g, w_in, b_gates, conv_w, conv_b, gmlp_norm_g, gmlp_w_spatial, gmlp_b_spatial, mlstm_norm_g, w_out, norm_ffn_g, peer_w_query, peer_sub_keys_1, peer_sub_keys_2, peer_expert_down, peer_expert_up, final_norm_g):
    bsz, seq, d = x.shape
    assert bsz == 1 and d == D_MODEL and seq % CHUNK == 0
    assert ada_w.shape[0] == 1
    t = bsz * seq
    xt = x.reshape(t, d)
    c8 = jnp.broadcast_to(c, (SUBLANES, d))
    pad = LANES - GATE_COLS
    tiles = _tiles(t)

    for l in range(1):
        mod = _ada(c8, ada_w[l], ada_b[l][None, :])[0:1]
        shift_m, scale_m, gate_m, shift_f, scale_f, gate_f = jnp.split(mod, N_MOD, axis=-1)

        w_bf = w_in[l][:, :PROJ_COLS].astype(BF16)
        w_gate = jnp.pad(w_in[l][:, PROJ_COLS:], ((0, 0), (0, pad)))
        b_gate = jnp.pad(b_gates[l], (0, pad))[None, :]
        proj, gates = _inproj(xt, norm_mix_g[l][None, :], scale_m, shift_m, w_bf, w_gate, b_gate,
                              **tiles["inproj"])

        yg = _gmlp(proj, gmlp_w_spatial[l], gmlp_b_spatial[l].T, gmlp_norm_g[l], **tiles["gmlp"])
        ym = _mlstm(proj, gates, conv_w[l], conv_b[l][None, :], mlstm_norm_g[l].reshape(1, D_MLSTM),
                    **tiles["mlstm"])

        x1, h2, s1t, s2t = _outproj(yg, ym, xt, w_out[l].astype(BF16), gate_m,
                                    norm_ffn_g[l][None, :], scale_f, shift_f,
                                    peer_w_query[l].astype(BF16), peer_sub_keys_1[l],
                                    peer_sub_keys_2[l], **tiles["outproj"])
        thrt, e1t, e2t = _route(s1t, s2t, **tiles["route"])
        fg = final_norm_g[None, :]
        xt = _peer(h2, peer_expert_down[l].astype(BF16), peer_expert_up[l].astype(BF16),
                   s2t, e2t, thrt, e1t, x1, gate_f, fg, **tiles["peer"])
    return xt.reshape(bsz, seq, d)
```

```python
import functools
import math

import jax
import jax.numpy as jnp
from jax import lax
from jax.experimental import pallas as pl
from jax.experimental.pallas import tpu as pltpu

D_MODEL = 2048
D_GMLP = 1024
GMLP_GROUPS = 8
GMLP_GROUP_DIM = 128
CHUNK = 128
D_MLSTM = 1024
MLSTM_HEADS = 4
MLSTM_HEAD_DIM = 256
CONV_WIDTH = 4
PEER_HEADS = 8
PEER_N_KEYS = 128
PEER_N_EXPERTS = PEER_N_KEYS * PEER_N_KEYS
PEER_HALF_DIM = 128
PEER_TOPK = 16
N_MOD = 6
EPS = 1e-6
PROJ_COLS = 2 * D_GMLP + 4 * D_MLSTM
GATE_COLS = 2 * MLSTM_HEADS
LANES = 128
SUBLANES = 8
VMEM_LIMIT = 56 * 1024 * 1024

F32 = jnp.float32
BF16 = jnp.bfloat16
NEG_INF = float("-inf")
POS_INF = float("inf")
HIGHEST = lax.Precision.HIGHEST


def _params(*sem):
    return pltpu.CompilerParams(dimension_semantics=sem, vmem_limit_bytes=VMEM_LIMIT)


def _gelu(x):
    return 0.5 * x * (1.0 + lax.erf(x * (1.0 / math.sqrt(2.0))))


def _sigmoid(x):
    return 1.0 / (1.0 + jnp.exp(-x))


def _rms(x):
    return x * lax.rsqrt(jnp.mean(x * x, axis=-1, keepdims=True) + EPS)


def _dot(a, b, **kw):
    return jnp.dot(a, b, preferred_element_type=F32, **kw)


def _dot_nt(a, b, **kw):
    return lax.dot_general(a, b, (((1,), (1,)), ((), ())), preferred_element_type=F32, **kw)


def _dot_tn(a, b, **kw):
    return lax.dot_general(a, b, (((0,), (0,)), ((), ())), preferred_element_type=F32, **kw)


def _ada_kernel(c_ref, w_ref, b_ref, o_ref):
    c = c_ref[...]
    sc = c * _sigmoid(c)
    o_ref[...] = _dot(sc, w_ref[...], precision=HIGHEST) + b_ref[...]


def _ada(c8, ada_w, ada_b):
    d, n = ada_w.shape
    tn = 1024
    return pl.pallas_call(
        _ada_kernel,
        grid=(n // tn,),
        in_specs=[pl.BlockSpec((SUBLANES, d), lambda j: (0, 0)),
                  pl.BlockSpec((d, tn), lambda j: (0, j)),
                  pl.BlockSpec((1, tn), lambda j: (0, j))],
        out_specs=pl.BlockSpec((SUBLANES, tn), lambda j: (0, j)),
        out_shape=jax.ShapeDtypeStruct((SUBLANES, n), F32),
        compiler_params=_params("arbitrary"),
        name="ada",
    )(c8, ada_w, ada_b)


def _inproj_kernel(x_ref, g_ref, sc_ref, sh_ref, w_ref, wg_ref, bg_ref, proj_ref, gate_ref, h_ref):
    @pl.when(pl.program_id(1) == 0)
    def _():
        h = _rms(x_ref[...]) * g_ref[...] * (1.0 + sc_ref[...]) + sh_ref[...]
        h_ref[...] = h.astype(BF16)
        gate_ref[...] = _dot(h_ref[...], wg_ref[...]) + bg_ref[...]

    proj_ref[...] = _dot(h_ref[...], w_ref[...])


def _inproj(x, g, scale, shift, w_bf, w_gate, b_gate, tm, tn):
    t, d = x.shape
    n = w_bf.shape[1]
    row = lambda i, j: (0, 0)
    return pl.pallas_call(
        _inproj_kernel,
        grid=(t // tm, n // tn),
        in_specs=[pl.BlockSpec((tm, d), lambda i, j: (i, 0)),
                  pl.BlockSpec((1, d), row), pl.BlockSpec((1, d), row), pl.BlockSpec((1, d), row),
                  pl.BlockSpec((d, tn), lambda i, j: (0, j)),
                  pl.BlockSpec((d, LANES), row), pl.BlockSpec((1, LANES), row)],
        out_specs=[pl.BlockSpec((tm, tn), lambda i, j: (i, j)),
                   pl.BlockSpec((tm, LANES), lambda i, j: (i, 0))],
        out_shape=[jax.ShapeDtypeStruct((t, n), F32), jax.ShapeDtypeStruct((t, LANES), F32)],
        scratch_shapes=[pltpu.VMEM((tm, d), BF16)],
        compiler_params=_params("parallel", "arbitrary"),
        name="inproj",
    )(x, g, scale, shift, w_bf, w_gate, b_gate)


def _gmlp_kernel(u_ref, v_ref, w_ref, bt_ref, ng_ref, o_ref, *, chunks):
    row = lax.broadcasted_iota(jnp.int32, (CHUNK, CHUNK), 0)
    col = lax.broadcasted_iota(jnp.int32, (CHUNK, CHUNK), 1)
    causal = row >= col
    for g in range(GMLP_GROUPS):
        w = jnp.where(causal, w_ref[g], 0.0).astype(BF16)
        bias = bt_ref[:, g:g + 1]
        ng = ng_ref[g:g + 1, :]
        cs = slice(g * GMLP_GROUP_DIM, (g + 1) * GMLP_GROUP_DIM)
        for n in range(chunks):
            rs = slice(n * CHUNK, (n + 1) * CHUNK)
            vn = _rms(_gelu(v_ref[rs, cs])) * ng
            mixed = _dot(w, vn.astype(BF16)) + bias
            o_ref[rs, cs] = (_gelu(u_ref[rs, cs]) * mixed).astype(BF16)


def _gmlp(proj, w_spatial, b_spatial_t, norm_g, tg):
    t = proj.shape[0]
    full = lambda *s: pl.BlockSpec(s, lambda i: (0,) * len(s))
    return pl.pallas_call(
        functools.partial(_gmlp_kernel, chunks=tg // CHUNK),
        grid=(t // tg,),
        in_specs=[pl.BlockSpec((tg, D_GMLP), lambda i: (i, 0)),
                  pl.BlockSpec((tg, D_GMLP), lambda i: (i, 1)),
                  full(GMLP_GROUPS, CHUNK, CHUNK), full(CHUNK, GMLP_GROUPS),
                  full(GMLP_GROUPS, GMLP_GROUP_DIM)],
        out_specs=pl.BlockSpec((tg, D_GMLP), lambda i: (i, 0)),
        out_shape=jax.ShapeDtypeStruct((t, D_GMLP), BF16),
        compiler_params=_params("parallel"),
        name="gmlp",
    )(proj, proj, w_spatial, b_spatial_t, norm_g)


def _mlstm_kernel(qk_ref, v_ref, o_ref, gate_ref, cw_ref, cb_ref, ng_ref, out_ref,
                  xext_ref, c_ref, n_ref, m_ref):
    L = CHUNK
    dh = MLSTM_HEAD_DIM

    @pl.when(pl.program_id(0) == 0)
    def _():
        xext_ref[0:SUBLANES, :] = jnp.zeros((SUBLANES, 2 * D_MLSTM), F32)
        c_ref[...] = jnp.zeros_like(c_ref)
        n_ref[...] = jnp.zeros_like(n_ref)
        m_ref[...] = jnp.zeros_like(m_ref)

    xext_ref[SUBLANES:SUBLANES + L, :] = qk_ref[...]
    conv = cb_ref[...]
    for k in range(CONV_WIDTH):
        off = SUBLANES - (CONV_WIDTH - 1) + k
        conv = conv + cw_ref[k:k + 1, :] * xext_ref[off:off + L, :]
    xext_ref[0:SUBLANES, :] = xext_ref[L:L + SUBLANES, :]
    qk = conv * _sigmoid(conv)

    gates = gate_ref[...]
    log_f = jnp.minimum(gates, 0.0) - jnp.log(1.0 + jnp.exp(-jnp.abs(gates)))
    row = lax.broadcasted_iota(jnp.int32, (L, L), 0)
    col = lax.broadcasted_iota(jnp.int32, (L, L), 1)
    causal = row >= col
    tri = jnp.where(causal, 1.0, 0.0).astype(F32)
    cum = _dot(tri, log_f, precision=HIGHEST)
    lane = lax.broadcasted_iota(jnp.int32, (L, LANES), 1)
    mat = jnp.where(lane < MLSTM_HEADS, gates, cum)
    mat_t = mat.T

    for h in range(MLSTM_HEADS):
        hs = slice(h * dh, (h + 1) * dh)
        q = (qk[:, h * dh:(h + 1) * dh] * (dh ** -0.5)).astype(BF16)
        k = qk[:, D_MLSTM + h * dh:D_MLSTM + (h + 1) * dh]
        v = v_ref[:, hs].astype(BF16)
        i_col = mat[:, h:h + 1]
        b_col = mat[:, MLSTM_HEADS + h:MLSTM_HEADS + h + 1]
        i_row = mat_t[h:h + 1, :]
        b_row = mat_t[MLSTM_HEADS + h:MLSTM_HEADS + h + 1, :]
        m_prev = m_ref[h][0:1, 0:1]
        c_prev = c_ref[h]
        n_prev = n_ref[h][0:1, :]

        log_d = jnp.where(causal, b_col - b_row + i_row, NEG_INF)
        a = b_col + m_prev
        m_comb = jnp.maximum(a, jnp.max(log_d, axis=-1, keepdims=True))
        w_intra = jnp.exp(log_d - m_comb)
        w_inter = jnp.exp(a - m_comb)
        s = _dot_nt(q, k.astype(BF16)) * w_intra
        num = _dot(s.astype(BF16), v) + w_inter * _dot(q, c_prev.astype(BF16))
        qf = q.astype(F32)
        den = (jnp.sum(s, axis=-1, keepdims=True)
               + w_inter * jnp.sum(qf * n_prev, axis=-1, keepdims=True))
        hout = num / jnp.maximum(jnp.abs(den), jnp.exp(-m_comb))

        b_last = b_col[L - 1:L, :]
        log_w = b_last - b_col + i_col
        m_new = jnp.maximum(b_last + m_prev, jnp.max(log_w, axis=0, keepdims=True))
        w_state = jnp.exp(log_w - m_new)
        decay = jnp.exp(b_last + m_prev - m_new)
        kw = w_state * k
        c_ref[h] = decay * c_prev + _dot_tn(kw.astype(BF16), v)
        n_ref[h] = jnp.broadcast_to(decay * n_prev + jnp.sum(kw, axis=0, keepdims=True),
                                    (SUBLANES, dh))
        m_ref[h] = jnp.broadcast_to(m_new, (SUBLANES, LANES))

        y = _rms(hout) * ng_ref[:, hs] * _sigmoid(o_ref[:, hs])
        out_ref[:, hs] = y.astype(BF16)


def _mlstm(proj, gates, conv_w, conv_b, norm_g):
    t = proj.shape[0]
    L = CHUNK
    full = lambda *s: pl.BlockSpec(s, lambda i: (0,) * len(s))
    return pl.pallas_call(
        _mlstm_kernel,
        grid=(t // L,),
        in_specs=[pl.BlockSpec((L, 2 * D_MLSTM), lambda i: (i, 1)),
                  pl.BlockSpec((L, D_MLSTM), lambda i: (i, 4)),
                  pl.BlockSpec((L, D_MLSTM), lambda i: (i, 5)),
                  pl.BlockSpec((L, LANES), lambda i: (i, 0)),
                  full(CONV_WIDTH, 2 * D_MLSTM), full(1, 2 * D_MLSTM), full(1, D_MLSTM)],
        out_specs=pl.BlockSpec((L, D_MLSTM), lambda i: (i, 0)),
        out_shape=jax.ShapeDtypeStruct((t, D_MLSTM), BF16),
        scratch_shapes=[pltpu.VMEM((L + SUBLANES, 2 * D_MLSTM), F32),
                        pltpu.VMEM((MLSTM_HEADS, MLSTM_HEAD_DIM, MLSTM_HEAD_DIM), F32),
                        pltpu.VMEM((MLSTM_HEADS, SUBLANES, MLSTM_HEAD_DIM), F32),
                        pltpu.VMEM((MLSTM_HEADS, SUBLANES, LANES), F32)],
        compiler_params=_params("arbitrary"),
        name="mlstm",
    )(proj, proj, proj, gates, conv_w, conv_b, norm_g)


def _outproj_kernel(yg_ref, ym_ref, x_ref, wo_ref, gm_ref, g_ref, sc_ref, sh_ref, wq_ref,
                    k1_ref, k2_ref, x1_ref, h2_ref, s1_ref, s2_ref):
    mix = _dot(yg_ref[...], wo_ref[0:D_GMLP, :]) + _dot(ym_ref[...], wo_ref[D_GMLP:, :])
    x1 = x_ref[...] + gm_ref[...] * mix
    x1_ref[...] = x1
    h2 = (_rms(x1) * g_ref[...] * (1.0 + sc_ref[...]) + sh_ref[...]).astype(BF16)
    h2_ref[...] = h2
    q = _dot(h2, wq_ref[...]).astype(BF16)
    k1 = k1_ref[...].astype(BF16)
    k2 = k2_ref[...].astype(BF16)
    for h in range(PEER_HEADS):
        base = h * 2 * PEER_HALF_DIM
        s1_ref[h] = _dot_nt(k1, q[:, base:base + PEER_HALF_DIM])
        s2_ref[h] = _dot_nt(k2, q[:, base + PEER_HALF_DIM:base + 2 * PEER_HALF_DIM])


def _outproj(yg, ym, x, w_out, gate_m, g, scale, shift, w_query, keys1, keys2, tm):
    t, d = x.shape
    row = lambda i: (0, 0)
    tok = lambda w: pl.BlockSpec((tm, w), lambda i: (i, 0))
    score = pl.BlockSpec((PEER_HEADS, PEER_N_KEYS, tm), lambda i: (0, 0, i))
    return pl.pallas_call(
        _outproj_kernel,
        grid=(t // tm,),
        in_specs=[tok(D_GMLP), tok(D_MLSTM), tok(d),
                  pl.BlockSpec((d, d), row, pipeline_mode=pl.Buffered(1)),
                  pl.BlockSpec((1, d), row), pl.BlockSpec((1, d), row),
                  pl.BlockSpec((1, d), row), pl.BlockSpec((1, d), row),
                  pl.BlockSpec((d, d), row, pipeline_mode=pl.Buffered(1)),
                  pl.BlockSpec((PEER_N_KEYS, PEER_HALF_DIM), row),
                  pl.BlockSpec((PEER_N_KEYS, PEER_HALF_DIM), row)],
        out_specs=[tok(d), tok(d), score, score],
        out_shape=[jax.ShapeDtypeStruct((t, d), F32), jax.ShapeDtypeStruct((t, d), BF16),
                   jax.ShapeDtypeStruct((PEER_HEADS, PEER_N_KEYS, t), F32),
                   jax.ShapeDtypeStruct((PEER_HEADS, PEER_N_KEYS, t), F32)],
        compiler_params=_params("parallel"),
        name="outproj",
    )(yg, ym, x, w_out, gate_m, g, scale, shift, w_query, keys1, keys2)


def _pair_candidates():
    return [(a, b) for a in range(PEER_TOPK) for b in range(PEER_TOPK)
            if (a + 1) * (b + 1) <= PEER_TOPK]


def _sort_network(n):
    def merge(lo, hi, r):
        step = r * 2
        if step < hi - lo:
            yield from merge(lo, hi, step)
            yield from merge(lo + r, hi, step)
            yield from [(i, i + r) for i in range(lo + r, hi - r, step)]
        else:
            yield (lo, lo + r)

    def sort(lo, hi):
        if hi - lo >= 1:
            mid = lo + (hi - lo) // 2
            yield from sort(lo, mid)
            yield from sort(mid + 1, hi)
            yield from merge(lo, hi, 1)

    return list(sort(0, n - 1))


def _compare_exchange(v, i, j):
    v[i], v[j] = jnp.maximum(v[i], v[j]), jnp.minimum(v[i], v[j])


def _bitonic_sort(v):
    d = PEER_TOPK // 2
    while d >= 1:
        for i in range(PEER_TOPK):
            if i & d == 0:
                _compare_exchange(v, i, i + d)
        d //= 2


def _top16(tiles):
    v = list(tiles)
    for i, j in _sort_network(PEER_TOPK):
        _compare_exchange(v, i, j)
    for shift in (4, 2, 1):
        other = [pltpu.roll(x, shift, axis=0) for x in v]
        v = [jnp.maximum(v[i], other[PEER_TOPK - 1 - i]) for i in range(PEER_TOPK)]
        _bitonic_sort(v)
    return v


def _route_kernel(s1_ref, s2_ref, thr_ref, e1_ref, e2_ref, *, cols):
    groups = PEER_N_KEYS // SUBLANES
    sub = lax.broadcasted_iota(jnp.int32, (SUBLANES, LANES), 0)
    pairs = _pair_candidates()

    def rows(g):
        return pl.ds(g * SUBLANES, SUBLANES)

    def column(c, carry):
        tok = pl.ds(pl.multiple_of(c * LANES, LANES), LANES)
        v1s = [jnp.zeros((SUBLANES, LANES), F32)] * PEER_TOPK
        v2s = [jnp.zeros((SUBLANES, LANES), F32)] * PEER_TOPK
        for h in range(PEER_HEADS):
            t1 = _top16([s1_ref[h, rows(g), tok] for g in range(groups)])
            t2 = _top16([s2_ref[h, rows(g), tok] for g in range(groups)])
            v1s = [jnp.where(sub == h, t1[a], v1s[a]) for a in range(PEER_TOPK)]
            v2s = [jnp.where(sub == h, t2[a], v2s[a]) for a in range(PEER_TOPK)]

        cands = [v1s[a] + v2s[b] for a, b in pairs]
        row = {ab: cnd for ab, cnd in zip(pairs, cands)}
        top = [row[(0, b)] for b in range(PEER_TOPK)]
        for a in range(1, PEER_TOPK):
            for b in range(PEER_TOPK // (a + 1)):
                i = PEER_TOPK - 1 - b
                top[i] = jnp.maximum(top[i], row[(a, b)])
            _bitonic_sort(top)
        tau = top[PEER_TOPK - 1]

        e2s = [jnp.exp(v2s[b] - v2s[0]) for b in range(PEER_TOPK)]
        thr_a = [jnp.full((SUBLANES, LANES), POS_INF, F32)] * PEER_TOPK
        zs_a = [jnp.zeros((SUBLANES, LANES), F32)] * PEER_TOPK
        for (a, b), cnd in zip(pairs, cands):
            hit = cnd >= tau
            thr_a[a] = jnp.where(hit, v2s[b], thr_a[a])
            zs_a[a] = zs_a[a] + jnp.where(hit, e2s[b], 0.0)
        z = jnp.zeros((SUBLANES, LANES), F32)
        for a in range(PEER_TOPK):
            z = z + jnp.exp(v1s[a] - v1s[0]) * zs_a[a]
        inv_z = 1.0 / z

        for h in range(PEER_HEADS):
            def rep(x):
                return jnp.broadcast_to(x[h:h + 1, :], (SUBLANES, LANES))
            v1r = [rep(v1s[a]) for a in range(PEER_TOPK)]
            thr_r = [rep(thr_a[a]) for a in range(PEER_TOPK)]
            max2 = rep(v2s[0])
            inv_zr = rep(inv_z)
            for g in range(groups):
                s1 = s1_ref[h, rows(g), tok]
                thr = jnp.full((SUBLANES, LANES), POS_INF, F32)
                for a in range(PEER_TOPK):
                    thr = jnp.where(s1 == v1r[a], thr_r[a], thr)
                thr_ref[h, rows(g), tok] = thr
                e1_ref[h, rows(g), tok] = jnp.exp(s1 - v1r[0]) * inv_zr
                e2_ref[h, rows(g), tok] = jnp.exp(s2_ref[h, rows(g), tok] - max2)
        return carry

    lax.fori_loop(0, cols, column, 0)


def _route(s1t, s2t, tc):
    t = s1t.shape[2]
    spec = pl.BlockSpec((PEER_HEADS, PEER_N_KEYS, tc), lambda i: (0, 0, i))
    shape = jax.ShapeDtypeStruct(s1t.shape, F32)
    return pl.pallas_call(
        functools.partial(_route_kernel, cols=tc // LANES),
        grid=(t // tc,),
        in_specs=[spec, spec],
        out_specs=[spec, spec, spec],
        out_shape=[shape, shape, shape],
        compiler_params=_params("parallel"),
        name="route",
    )(s1t, s2t)


def _peer_kernel(h2_ref, down_ref, up_ref, s2_ref, e2_ref, thr_ref, e1_ref, x1_ref, gf_ref, fg_ref,
                 out_ref, a0_ref, a1_ref, w0_ref, w1_ref, *, tb, nj, steps):
    n = pl.program_id(0)
    j_b = jnp.clip(n - 1, 0, steps - 1) % nj
    j_c = jnp.clip(n - 2, 0, steps - 1) % nj
    groups = PEER_N_KEYS // SUBLANES

    @pl.when(n == 0)
    def _():
        a1_ref[...] = jnp.zeros_like(a1_ref)
        w0_ref[...] = jnp.zeros_like(w0_ref)
        w1_ref[...] = jnp.zeros_like(w1_ref)

    @pl.when(j_c == 0)
    def _():
        out_ref[...] = jnp.zeros_like(out_ref)

    rows = pl.ds(pl.multiple_of(j_b * SUBLANES, SUBLANES), SUBLANES)

    def column(c, a_ref, w_ref):
        tok = pl.ds(c * LANES, LANES)
        thr8 = [thr_ref[h, rows, tok] for h in range(PEER_HEADS)]
        e18 = [e1_ref[h, rows, tok] for h in range(PEER_HEADS)]
        for s in range(SUBLANES):
            acc = [jnp.zeros((SUBLANES, LANES), F32) for _ in range(groups)]
            for h in range(PEER_HEADS):
                thr = jnp.broadcast_to(thr8[h][s:s + 1, :], (SUBLANES, LANES))
                e1 = jnp.broadcast_to(e18[h][s:s + 1, :], (SUBLANES, LANES))
                for g in range(groups):
                    ks = pl.ds(g * SUBLANES, SUBLANES)
                    sel = jnp.where(s2_ref[h, ks, tok] >= thr, e2_ref[h, ks, tok], 0.0)
                    acc[g] = acc[g] + sel * e1
            gt = jnp.concatenate(acc, axis=0)
            es = slice(s * PEER_N_KEYS, (s + 1) * PEER_N_KEYS)
            w_ref[tok, es] = (_gelu(a_ref[tok, es]) * gt.T).astype(BF16)

    def stages(a_write, a_read, w_write, w_read):
        a_write[...] = _dot_nt(h2_ref[...], down_ref[...])
        for c in range(tb // LANES):
            column(c, a_read, w_write)
        out_ref[...] += _dot(w_read[...], up_ref[...])

    @pl.when(n % 2 == 0)
    def _():
        stages(a0_ref, a1_ref, w1_ref, w0_ref)

    @pl.when(n % 2 == 1)
    def _():
        stages(a1_ref, a0_ref, w0_ref, w1_ref)

    @pl.when(j_c == nj - 1)
    def _():
        x2 = x1_ref[...] + gf_ref[...] * out_ref[...]
        out_ref[...] = _rms(x2) * fg_ref[...]


def _peer(h2, down_bf, up_bf, s2t, e2t, thrt, e1t, x1, gate_f, final_g, tb):
    t, d = x1.shape
    se = SUBLANES * PEER_N_KEYS
    nj = PEER_N_EXPERTS // se
    steps = (t // tb) * nj
    stage_a = lambda n: jnp.minimum(n, steps - 1)
    stage_b = lambda n: jnp.clip(n - 1, 0, steps - 1)
    stage_c = lambda n: jnp.clip(n - 2, 0, steps - 1)
    row = lambda n: (0, 0)
    route = lambda: pl.BlockSpec((PEER_HEADS, PEER_N_KEYS, tb), lambda n: (0, 0, stage_b(n) // nj),
                                 pipeline_mode=pl.Buffered(1))
    return pl.pallas_call(
        functools.partial(_peer_kernel, tb=tb, nj=nj, steps=steps),
        grid=(steps + 2,),
        in_specs=[pl.BlockSpec((tb, d), lambda n: (stage_a(n) // nj, 0)),
                  pl.BlockSpec((se, d), lambda n: (stage_a(n) % nj, 0)),
                  pl.BlockSpec((se, d), lambda n: (stage_c(n) % nj, 0)),
                  route(), route(), route(), route(),
                  pl.BlockSpec((tb, d), lambda n: (stage_c(n) // nj, 0),
                               pipeline_mode=pl.Buffered(1)),
                  pl.BlockSpec((1, d), row), pl.BlockSpec((1, d), row)],
        out_specs=pl.BlockSpec((tb, d), lambda n: (stage_c(n) // nj, 0)),
        out_shape=jax.ShapeDtypeStruct((t, d), F32),
        scratch_shapes=[pltpu.VMEM((tb, se), F32), pltpu.VMEM((tb, se), F32),
                        pltpu.VMEM((tb, se), BF16), pltpu.VMEM((tb, se), BF16)],
        compiler_params=_params("arbitrary"),
        name="peer",
    )(h2, down_bf, up_bf, s2t, e2t, thrt, e1t, x1, gate_f, final_g)


def _pick(n, *cands):
    for cnd in cands:
        if n % cnd == 0:
            return cnd
    raise ValueError(f"unsupported token count {n}")


def _tiles(t):
    return {
        "inproj": dict(tm=_pick(t, 1024, 512, 256, 128), tn=1024),
        "gmlp": dict(tg=_pick(t, 512, 256, 128)),
        "mlstm": dict(),
        "outproj": dict(tm=_pick(t, 256, 128)),
        "route": dict(tc=_pick(t, 512, 256, 128)),
        "peer": dict(tb=_pick(t, 512, 256, 128)),
    }


def kernel(x, c, ada_w, ada_b, norm_mix_g, w_in, b_gates, conv_w, conv_b, gmlp_norm_g, gmlp_w_spatial, gmlp_b_spatial, mlstm_norm_g, w_out, norm_ffn_g, peer_w_query, peer_sub_keys_1, peer_sub_keys_2, peer_expert_down, peer_expert_up, final_norm_g):
    bsz, seq, d = x.shape
    assert bsz == 1 and d == D_MODEL and seq % CHUNK == 0
    assert ada_w.shape[0] == 1
    t = bsz * seq
    xt = x.reshape(t, d)
    c8 = jnp.broadcast_to(c, (SUBLANES, d))
    pad = LANES - GATE_COLS
    tiles = _tiles(t)

    for l in range(1):
        mod = _ada(c8, ada_w[l], ada_b[l][None, :])[0:1]
        shift_m, scale_m, gate_m, shift_f, scale_f, gate_f = jnp.split(mod, N_MOD, axis=-1)

        w_bf = w_in[l][:, :PROJ_COLS].astype(BF16)
        w_gate = jnp.pad(w_in[l][:, PROJ_COLS:], ((0, 0), (0, pad))).astype(BF16)
        b_gate = jnp.pad(b_gates[l], (0, pad))[None, :]
        proj, gates = _inproj(xt, norm_mix_g[l][None, :], scale_m, shift_m, w_bf, w_gate, b_gate,
                              **tiles["inproj"])

        yg = _gmlp(proj, gmlp_w_spatial[l], gmlp_b_spatial[l].T, gmlp_norm_g[l], **tiles["gmlp"])
        ym = _mlstm(proj, gates, conv_w[l], conv_b[l][None, :], mlstm_norm_g[l].reshape(1, D_MLSTM),
                    **tiles["mlstm"])

        x1, h2, s1t, s2t = _outproj(yg, ym, xt, w_out[l].astype(BF16), gate_m,
                                    norm_ffn_g[l][None, :], scale_f, shift_f,
                                    peer_w_query[l].astype(BF16), peer_sub_keys_1[l],
                                    peer_sub_keys_2[l], **tiles["outproj"])
        thrt, e1t, e2t = _route(s1t, s2t, **tiles["route"])
        fg = final_norm_g[None, :]
        xt = _peer(h2, peer_expert_down[l].astype(BF16), peer_expert_up[l].astype(BF16),
                   s2t, e2t, thrt, e1t, x1, gate_f, fg, **tiles["peer"])
    return xt.reshape(bsz, seq, d)
```

```python
import functools
import math

import jax
import jax.numpy as jnp
from jax import lax
from jax.experimental import pallas as pl
from jax.experimental.pallas import tpu as pltpu

D_MODEL = 2048
D_GMLP = 1024
GMLP_GROUPS = 8
GMLP_GROUP_DIM = 128
CHUNK = 128
D_MLSTM = 1024
MLSTM_HEADS = 4
MLSTM_HEAD_DIM = 256
CONV_WIDTH = 4
PEER_HEADS = 8
PEER_N_KEYS = 128
PEER_N_EXPERTS = PEER_N_KEYS * PEER_N_KEYS
PEER_HALF_DIM = 128
PEER_TOPK = 16
N_MOD = 6
EPS = 1e-6
PROJ_COLS = 2 * D_GMLP + 4 * D_MLSTM
GATE_COLS = 2 * MLSTM_HEADS
LANES = 128
SUBLANES = 8
VMEM_LIMIT = 56 * 1024 * 1024
PEER_VMEM_LIMIT = 62 * 1024 * 1024

F32 = jnp.float32
BF16 = jnp.bfloat16
NEG_INF = float("-inf")
POS_INF = float("inf")
HIGHEST = lax.Precision.HIGHEST


def _params(*sem):
    return pltpu.CompilerParams(dimension_semantics=sem, vmem_limit_bytes=VMEM_LIMIT)


def _gelu(x):
    return 0.5 * x * (1.0 + lax.erf(x * (1.0 / math.sqrt(2.0))))


def _sigmoid(x):
    return 1.0 / (1.0 + jnp.exp(-x))


def _rms(x):
    return x * lax.rsqrt(jnp.mean(x * x, axis=-1, keepdims=True) + EPS)


def _dot(a, b, **kw):
    return jnp.dot(a, b, preferred_element_type=F32, **kw)


def _dot_nt(a, b, **kw):
    return lax.dot_general(a, b, (((1,), (1,)), ((), ())), preferred_element_type=F32, **kw)


def _dot_tn(a, b, **kw):
    return lax.dot_general(a, b, (((0,), (0,)), ((), ())), preferred_element_type=F32, **kw)


def _ada_kernel(c_ref, w_ref, b_ref, o_ref):
    c = c_ref[...]
    sc = c * _sigmoid(c)
    o_ref[...] = _dot(sc, w_ref[...], precision=HIGHEST) + b_ref[...]


def _ada(c8, ada_w, ada_b):
    d, n = ada_w.shape
    tn = 1024
    return pl.pallas_call(
        _ada_kernel,
        grid=(n // tn,),
        in_specs=[pl.BlockSpec((SUBLANES, d), lambda j: (0, 0)),
                  pl.BlockSpec((d, tn), lambda j: (0, j)),
                  pl.BlockSpec((1, tn), lambda j: (0, j))],
        out_specs=pl.BlockSpec((SUBLANES, tn), lambda j: (0, j)),
        out_shape=jax.ShapeDtypeStruct((SUBLANES, n), F32),
        compiler_params=_params("arbitrary"),
        name="ada",
    )(c8, ada_w, ada_b)


def _inproj_kernel(x_ref, g_ref, sc_ref, sh_ref, w_ref, wg_ref, bg_ref, proj_ref, gate_ref, h_ref):
    @pl.when(pl.program_id(1) == 0)
    def _():
        h = _rms(x_ref[...]) * g_ref[...] * (1.0 + sc_ref[...]) + sh_ref[...]
        h_ref[...] = h.astype(BF16)
        gate_ref[...] = _dot(h_ref[...], wg_ref[...]) + bg_ref[...]

    proj_ref[...] = _dot(h_ref[...], w_ref[...])


def _inproj(x, g, scale, shift, w_bf, w_gate, b_gate, tm, tn):
    t, d = x.shape
    n = w_bf.shape[1]
    row = lambda i, j: (0, 0)
    return pl.pallas_call(
        _inproj_kernel,
        grid=(t // tm, n // tn),
        in_specs=[pl.BlockSpec((tm, d), lambda i, j: (i, 0)),
                  pl.BlockSpec((1, d), row), pl.BlockSpec((1, d), row), pl.BlockSpec((1, d), row),
                  pl.BlockSpec((d, tn), lambda i, j: (0, j)),
                  pl.BlockSpec((d, LANES), row), pl.BlockSpec((1, LANES), row)],
        out_specs=[pl.BlockSpec((tm, tn), lambda i, j: (i, j)),
                   pl.BlockSpec((tm, LANES), lambda i, j: (i, 0))],
        out_shape=[jax.ShapeDtypeStruct((t, n), F32), jax.ShapeDtypeStruct((t, LANES), F32)],
        scratch_shapes=[pltpu.VMEM((tm, d), BF16)],
        compiler_params=_params("parallel", "arbitrary"),
        name="inproj",
    )(x, g, scale, shift, w_bf, w_gate, b_gate)


def _gmlp_kernel(u_ref, v_ref, w_ref, bt_ref, ng_ref, o_ref, *, chunks):
    row = lax.broadcasted_iota(jnp.int32, (CHUNK, CHUNK), 0)
    col = lax.broadcasted_iota(jnp.int32, (CHUNK, CHUNK), 1)
    causal = row >= col
    for g in range(GMLP_GROUPS):
        w = jnp.where(causal, w_ref[g], 0.0).astype(BF16)
        bias = bt_ref[:, g:g + 1]
        ng = ng_ref[g:g + 1, :]
        cs = slice(g * GMLP_GROUP_DIM, (g + 1) * GMLP_GROUP_DIM)
        for n in range(chunks):
            rs = slice(n * CHUNK, (n + 1) * CHUNK)
            vn = _rms(_gelu(v_ref[rs, cs])) * ng
            mixed = _dot(w, vn.astype(BF16)) + bias
            o_ref[rs, cs] = (_gelu(u_ref[rs, cs]) * mixed).astype(BF16)


def _gmlp(proj, w_spatial, b_spatial_t, norm_g, tg):
    t = proj.shape[0]
    full = lambda *s: pl.BlockSpec(s, lambda i: (0,) * len(s))
    return pl.pallas_call(
        functools.partial(_gmlp_kernel, chunks=tg // CHUNK),
        grid=(t // tg,),
        in_specs=[pl.BlockSpec((tg, D_GMLP), lambda i: (i, 0)),
                  pl.BlockSpec((tg, D_GMLP), lambda i: (i, 1)),
                  full(GMLP_GROUPS, CHUNK, CHUNK), full(CHUNK, GMLP_GROUPS),
                  full(GMLP_GROUPS, GMLP_GROUP_DIM)],
        out_specs=pl.BlockSpec((tg, D_GMLP), lambda i: (i, 0)),
        out_shape=jax.ShapeDtypeStruct((t, D_GMLP), BF16),
        compiler_params=_params("parallel"),
        name="gmlp",
    )(proj, proj, w_spatial, b_spatial_t, norm_g)


def _mlstm_kernel(qk_ref, v_ref, o_ref, gate_ref, cw_ref, cb_ref, ng_ref, out_ref,
                  xext_ref, c_ref, n_ref, m_ref):
    L = CHUNK
    dh = MLSTM_HEAD_DIM

    @pl.when(pl.program_id(0) == 0)
    def _():
        xext_ref[0:SUBLANES, :] = jnp.zeros((SUBLANES, 2 * D_MLSTM), F32)
        c_ref[...] = jnp.zeros_like(c_ref)
        n_ref[...] = jnp.zeros_like(n_ref)
        m_ref[...] = jnp.zeros_like(m_ref)

    xext_ref[SUBLANES:SUBLANES + L, :] = qk_ref[...]
    conv = cb_ref[...]
    for k in range(CONV_WIDTH):
        off = SUBLANES - (CONV_WIDTH - 1) + k
        conv = conv + cw_ref[k:k + 1, :] * xext_ref[off:off + L, :]
    xext_ref[0:SUBLANES, :] = xext_ref[L:L + SUBLANES, :]
    qk = conv * _sigmoid(conv)

    gates = gate_ref[...]
    log_f = jnp.minimum(gates, 0.0) - jnp.log(1.0 + jnp.exp(-jnp.abs(gates)))
    row = lax.broadcasted_iota(jnp.int32, (L, L), 0)
    col = lax.broadcasted_iota(jnp.int32, (L, L), 1)
    causal = row >= col
    tri = jnp.where(causal, 1.0, 0.0).astype(F32)
    cum = _dot(tri, log_f, precision=HIGHEST)
    lane = lax.broadcasted_iota(jnp.int32, (L, LANES), 1)
    mat = jnp.where(lane < MLSTM_HEADS, gates, cum)
    mat_t = mat.T

    for h in range(MLSTM_HEADS):
        hs = slice(h * dh, (h + 1) * dh)
        q = (qk[:, h * dh:(h + 1) * dh] * (dh ** -0.5)).astype(BF16)
        k = qk[:, D_MLSTM + h * dh:D_MLSTM + (h + 1) * dh]
        v = v_ref[:, hs].astype(BF16)
        i_col = mat[:, h:h + 1]
        b_col = mat[:, MLSTM_HEADS + h:MLSTM_HEADS + h + 1]
        i_row = mat_t[h:h + 1, :]
        b_row = mat_t[MLSTM_HEADS + h:MLSTM_HEADS + h + 1, :]
        m_prev = m_ref[h][0:1, 0:1]
        c_prev = c_ref[h]
        n_prev = n_ref[h][0:1, :]

        log_d = jnp.where(causal, b_col - b_row + i_row, NEG_INF)
        a = b_col + m_prev
        m_comb = jnp.maximum(a, jnp.max(log_d, axis=-1, keepdims=True))
        w_intra = jnp.exp(log_d - m_comb)
        w_inter = jnp.exp(a - m_comb)
        s = _dot_nt(q, k.astype(BF16)) * w_intra
        num = _dot(s.astype(BF16), v) + w_inter * _dot(q, c_prev.astype(BF16))
        qf = q.astype(F32)
        den = (jnp.sum(s, axis=-1, keepdims=True)
               + w_inter * jnp.sum(qf * n_prev, axis=-1, keepdims=True))
        hout = num / jnp.maximum(jnp.abs(den), jnp.exp(-m_comb))

        b_last = b_col[L - 1:L, :]
        log_w = b_last - b_col + i_col
        m_new = jnp.maximum(b_last + m_prev, jnp.max(log_w, axis=0, keepdims=True))
        w_state = jnp.exp(log_w - m_new)
        decay = jnp.exp(b_last + m_prev - m_new)
        kw = w_state * k
        c_ref[h] = decay * c_prev + _dot_tn(kw.astype(BF16), v)
        n_ref[h] = jnp.broadcast_to(decay * n_prev + jnp.sum(kw, axis=0, keepdims=True),
                                    (SUBLANES, dh))
        m_ref[h] = jnp.broadcast_to(m_new, (SUBLANES, LANES))

        y = _rms(hout) * ng_ref[:, hs] * _sigmoid(o_ref[:, hs])
        out_ref[:, hs] = y.astype(BF16)


def _mlstm(proj, gates, conv_w, conv_b, norm_g):
    t = proj.shape[0]
    L = CHUNK
    full = lambda *s: pl.BlockSpec(s, lambda i: (0,) * len(s))
    return pl.pallas_call(
        _mlstm_kernel,
        grid=(t // L,),
        in_specs=[pl.BlockSpec((L, 2 * D_MLSTM), lambda i: (i, 1)),
                  pl.BlockSpec((L, D_MLSTM), lambda i: (i, 4)),
                  pl.BlockSpec((L, D_MLSTM), lambda i: (i, 5)),
                  pl.BlockSpec((L, LANES), lambda i: (i, 0)),
                  full(CONV_WIDTH, 2 * D_MLSTM), full(1, 2 * D_MLSTM), full(1, D_MLSTM)],
        out_specs=pl.BlockSpec((L, D_MLSTM), lambda i: (i, 0)),
        out_shape=jax.ShapeDtypeStruct((t, D_MLSTM), BF16),
        scratch_shapes=[pltpu.VMEM((L + SUBLANES, 2 * D_MLSTM), F32),
                        pltpu.VMEM((MLSTM_HEADS, MLSTM_HEAD_DIM, MLSTM_HEAD_DIM), F32),
                        pltpu.VMEM((MLSTM_HEADS, SUBLANES, MLSTM_HEAD_DIM), F32),
                        pltpu.VMEM((MLSTM_HEADS, SUBLANES, LANES), F32)],
        compiler_params=_params("arbitrary"),
        name="mlstm",
    )(proj, proj, proj, gates, conv_w, conv_b, norm_g)


def _outproj_kernel(yg_ref, ym_ref, x_ref, wo_ref, gm_ref, g_ref, sc_ref, sh_ref, wq_ref,
                    k1_ref, k2_ref, x1_ref, h2_ref, s1_ref, s2_ref):
    mix = _dot(yg_ref[...], wo_ref[0:D_GMLP, :]) + _dot(ym_ref[...], wo_ref[D_GMLP:, :])
    x1 = x_ref[...] + gm_ref[...] * mix
    x1_ref[...] = x1
    h2 = (_rms(x1) * g_ref[...] * (1.0 + sc_ref[...]) + sh_ref[...]).astype(BF16)
    h2_ref[...] = h2
    q = _dot(h2, wq_ref[...]).astype(BF16)
    k1 = k1_ref[...].astype(BF16)
    k2 = k2_ref[...].astype(BF16)
    for h in range(PEER_HEADS):
        base = h * 2 * PEER_HALF_DIM
        s1_ref[h] = _dot_nt(k1, q[:, base:base + PEER_HALF_DIM])
        s2_ref[h] = _dot_nt(k2, q[:, base + PEER_HALF_DIM:base + 2 * PEER_HALF_DIM])


def _outproj(yg, ym, x, w_out, gate_m, g, scale, shift, w_query, keys1, keys2, tm):
    t, d = x.shape
    row = lambda i: (0, 0)
    tok = lambda w: pl.BlockSpec((tm, w), lambda i: (i, 0))
    score = pl.BlockSpec((PEER_HEADS, PEER_N_KEYS, tm), lambda i: (0, 0, i))
    return pl.pallas_call(
        _outproj_kernel,
        grid=(t // tm,),
        in_specs=[tok(D_GMLP), tok(D_MLSTM), tok(d),
                  pl.BlockSpec((d, d), row, pipeline_mode=pl.Buffered(1)),
                  pl.BlockSpec((1, d), row), pl.BlockSpec((1, d), row),
                  pl.BlockSpec((1, d), row), pl.BlockSpec((1, d), row),
                  pl.BlockSpec((d, d), row, pipeline_mode=pl.Buffered(1)),
                  pl.BlockSpec((PEER_N_KEYS, PEER_HALF_DIM), row),
                  pl.BlockSpec((PEER_N_KEYS, PEER_HALF_DIM), row)],
        out_specs=[tok(d), tok(d), score, score],
        out_shape=[jax.ShapeDtypeStruct((t, d), F32), jax.ShapeDtypeStruct((t, d), BF16),
                   jax.ShapeDtypeStruct((PEER_HEADS, PEER_N_KEYS, t), F32),
                   jax.ShapeDtypeStruct((PEER_HEADS, PEER_N_KEYS, t), F32)],
        compiler_params=_params("parallel"),
        name="outproj",
    )(yg, ym, x, w_out, gate_m, g, scale, shift, w_query, keys1, keys2)


def _pair_candidates():
    return [(a, b) for a in range(PEER_TOPK) for b in range(PEER_TOPK)
            if (a + 1) * (b + 1) <= PEER_TOPK]


def _sort_network(n):
    def merge(lo, hi, r):
        step = r * 2
        if step < hi - lo:
            yield from merge(lo, hi, step)
            yield from merge(lo + r, hi, step)
            yield from [(i, i + r) for i in range(lo + r, hi - r, step)]
        else:
            yield (lo, lo + r)

    def sort(lo, hi):
        if hi - lo >= 1:
            mid = lo + (hi - lo) // 2
            yield from sort(lo, mid)
            yield from sort(mid + 1, hi)
            yield from merge(lo, hi, 1)

    return list(sort(0, n - 1))


def _compare_exchange(v, i, j):
    v[i], v[j] = jnp.maximum(v[i], v[j]), jnp.minimum(v[i], v[j])


def _bitonic_sort(v):
    d = PEER_TOPK // 2
    while d >= 1:
        for i in range(PEER_TOPK):
            if i & d == 0:
                _compare_exchange(v, i, i + d)
        d //= 2


def _top16(tiles):
    v = list(tiles)
    for i, j in _sort_network(PEER_TOPK):
        _compare_exchange(v, i, j)
    for shift in (4, 2, 1):
        other = [pltpu.roll(x, shift, axis=0) for x in v]
        v = [jnp.maximum(v[i], other[PEER_TOPK - 1 - i]) for i in range(PEER_TOPK)]
        _bitonic_sort(v)
    return v


def _route_kernel(s1_ref, s2_ref, thr_ref, e1_ref, e2_ref, *, cols):
    groups = PEER_N_KEYS // SUBLANES
    sub = lax.broadcasted_iota(jnp.int32, (SUBLANES, LANES), 0)
    pairs = _pair_candidates()

    def rows(g):
        return pl.ds(g * SUBLANES, SUBLANES)

    def column(c, carry):
        tok = pl.ds(pl.multiple_of(c * LANES, LANES), LANES)
        v1s = [jnp.zeros((SUBLANES, LANES), F32)] * PEER_TOPK
        v2s = [jnp.zeros((SUBLANES, LANES), F32)] * PEER_TOPK
        for h in range(PEER_HEADS):
            t1 = _top16([s1_ref[h, rows(g), tok] for g in range(groups)])
            t2 = _top16([s2_ref[h, rows(g), tok] for g in range(groups)])
            v1s = [jnp.where(sub == h, t1[a], v1s[a]) for a in range(PEER_TOPK)]
            v2s = [jnp.where(sub == h, t2[a], v2s[a]) for a in range(PEER_TOPK)]

        cands = [v1s[a] + v2s[b] for a, b in pairs]
        row = {ab: cnd for ab, cnd in zip(pairs, cands)}
        top = [row[(0, b)] for b in range(PEER_TOPK)]
        for a in range(1, PEER_TOPK):
            for b in range(PEER_TOPK // (a + 1)):
                i = PEER_TOPK - 1 - b
                top[i] = jnp.maximum(top[i], row[(a, b)])
            _bitonic_sort(top)
        tau = top[PEER_TOPK - 1]

        e2s = [jnp.exp(v2s[b] - v2s[0]) for b in range(PEER_TOPK)]
        thr_a = [jnp.full((SUBLANES, LANES), POS_INF, F32)] * PEER_TOPK
        zs_a = [jnp.zeros((SUBLANES, LANES), F32)] * PEER_TOPK
        for (a, b), cnd in zip(pairs, cands):
            hit = cnd >= tau
            thr_a[a] = jnp.where(hit, v2s[b], thr_a[a])
            zs_a[a] = zs_a[a] + jnp.where(hit, e2s[b], 0.0)
        z = jnp.zeros((SUBLANES, LANES), F32)
        for a in range(PEER_TOPK):
            z = z + jnp.exp(v1s[a] - v1s[0]) * zs_a[a]
        inv_z = 1.0 / z

        for h in range(PEER_HEADS):
            def rep(x):
                return jnp.broadcast_to(x[h:h + 1, :], (SUBLANES, LANES))
            v1r = [rep(v1s[a]) for a in range(PEER_TOPK)]
            thr_r = [rep(thr_a[a]) for a in range(PEER_TOPK)]
            max2 = rep(v2s[0])
            inv_zr = rep(inv_z)
            for g in range(groups):
                s1 = s1_ref[h, rows(g), tok]
                thr = jnp.full((SUBLANES, LANES), POS_INF, F32)
                for a in range(PEER_TOPK):
                    thr = jnp.where(s1 == v1r[a], thr_r[a], thr)
                thr_ref[h, rows(g), tok] = thr
                e1_ref[h, rows(g), tok] = jnp.exp(s1 - v1r[0]) * inv_zr
                e2_ref[h, rows(g), tok] = jnp.exp(s2_ref[h, rows(g), tok] - max2)
        return carry

    lax.fori_loop(0, cols, column, 0)


def _route(s1t, s2t, tc):
    t = s1t.shape[2]
    spec = pl.BlockSpec((PEER_HEADS, PEER_N_KEYS, tc), lambda i: (0, 0, i))
    shape = jax.ShapeDtypeStruct(s1t.shape, F32)
    return pl.pallas_call(
        functools.partial(_route_kernel, cols=tc // LANES),
        grid=(t // tc,),
        in_specs=[spec, spec],
        out_specs=[spec, spec, spec],
        out_shape=[shape, shape, shape],
        compiler_params=_params("parallel"),
        name="route",
    )(s1t, s2t)


def _peer_kernel(h2_ref, down_ref, up_ref, s2_ref, e2_ref, thr_ref, e1_ref, x1_ref, gf_ref, fg_ref,
                 out_ref, a0_ref, a1_ref, w0_ref, w1_ref, *, tb, nj, steps):
    n = pl.program_id(0)
    j_b = jnp.clip(n - 1, 0, steps - 1) % nj
    j_c = jnp.clip(n - 2, 0, steps - 1) % nj
    groups = PEER_N_KEYS // SUBLANES

    @pl.when(n == 0)
    def _():
        a1_ref[...] = jnp.zeros_like(a1_ref)
        w0_ref[...] = jnp.zeros_like(w0_ref)
        w1_ref[...] = jnp.zeros_like(w1_ref)

    @pl.when(j_c == 0)
    def _():
        out_ref[...] = jnp.zeros_like(out_ref)

    rows = pl.ds(pl.multiple_of(j_b * SUBLANES, SUBLANES), SUBLANES)

    def column(c, a_ref, w_ref):
        tok = pl.ds(c * LANES, LANES)
        thr8 = [thr_ref[h, rows, tok] for h in range(PEER_HEADS)]
        e18 = [e1_ref[h, rows, tok] for h in range(PEER_HEADS)]
        for s in range(SUBLANES):
            acc = [jnp.zeros((SUBLANES, LANES), F32) for _ in range(groups)]
            for h in range(PEER_HEADS):
                thr = jnp.broadcast_to(thr8[h][s:s + 1, :], (SUBLANES, LANES))
                e1 = jnp.broadcast_to(e18[h][s:s + 1, :], (SUBLANES, LANES))
                for g in range(groups):
                    ks = pl.ds(g * SUBLANES, SUBLANES)
                    sel = jnp.where(s2_ref[h, ks, tok] >= thr, e2_ref[h, ks, tok], 0.0)
                    acc[g] = acc[g] + sel * e1
            gt = jnp.concatenate(acc, axis=0)
            es = slice(s * PEER_N_KEYS, (s + 1) * PEER_N_KEYS)
            w_ref[tok, es] = (_gelu(a_ref[tok, es]) * gt.T).astype(BF16)

    def stages(a_write, a_read, w_write, w_read):
        a_write[...] = _dot(h2_ref[...], down_ref[...])
        for c in range(tb // LANES):
            column(c, a_read, w_write)
        out_ref[...] += _dot(w_read[...], up_ref[...])

    @pl.when(n % 2 == 0)
    def _():
        stages(a0_ref, a1_ref, w1_ref, w0_ref)

    @pl.when(n % 2 == 1)
    def _():
        stages(a1_ref, a0_ref, w0_ref, w1_ref)

    @pl.when(j_c == nj - 1)
    def _():
        x2 = x1_ref[...] + gf_ref[...] * out_ref[...]
        out_ref[...] = _rms(x2) * fg_ref[...]


def _peer(h2, down_t, up_bf, s2t, e2t, thrt, e1t, x1, gate_f, final_g, tb):
    t, d = x1.shape
    se = SUBLANES * PEER_N_KEYS
    nj = PEER_N_EXPERTS // se
    steps = (t // tb) * nj
    stage_a = lambda n: jnp.minimum(n, steps - 1)
    stage_b = lambda n: jnp.clip(n - 1, 0, steps - 1)
    stage_c = lambda n: jnp.clip(n - 2, 0, steps - 1)
    row = lambda n: (0, 0)
    route = lambda: pl.BlockSpec((PEER_HEADS, PEER_N_KEYS, tb), lambda n: (0, 0, stage_b(n) // nj))
    return pl.pallas_call(
        functools.partial(_peer_kernel, tb=tb, nj=nj, steps=steps),
        grid=(steps + 2,),
        in_specs=[pl.BlockSpec((tb, d), lambda n: (stage_a(n) // nj, 0)),
                  pl.BlockSpec((d, se), lambda n: (0, stage_a(n) % nj)),
                  pl.BlockSpec((se, d), lambda n: (stage_c(n) % nj, 0)),
                  route(), route(), route(), route(),
                  pl.BlockSpec((tb, d), lambda n: (stage_c(n) // nj, 0),
                               pipeline_mode=pl.Buffered(1)),
                  pl.BlockSpec((1, d), row), pl.BlockSpec((1, d), row)],
        out_specs=pl.BlockSpec((tb, d), lambda n: (stage_c(n) // nj, 0)),
        out_shape=jax.ShapeDtypeStruct((t, d), F32),
        scratch_shapes=[pltpu.VMEM((tb, se), F32), pltpu.VMEM((tb, se), F32),
                        pltpu.VMEM((tb, se), BF16), pltpu.VMEM((tb, se), BF16)],
        compiler_params=pltpu.CompilerParams(dimension_semantics=("arbitrary",),
                                             vmem_limit_bytes=PEER_VMEM_LIMIT),
        name="peer",
    )(h2, down_t, up_bf, s2t, e2t, thrt, e1t, x1, gate_f, final_g)


def _pick(n, *cands):
    for cnd in cands:
        if n % cnd == 0:
            return cnd
    raise ValueError(f"unsupported token count {n}")


def _tiles(t):
    return {
        "inproj": dict(tm=_pick(t, 1024, 512, 256, 128), tn=1024),
        "gmlp": dict(tg=_pick(t, 512, 256, 128)),
        "mlstm": dict(),
        "outproj": dict(tm=_pick(t, 256, 128)),
        "route": dict(tc=_pick(t, 512, 256, 128)),
        "peer": dict(tb=_pick(t, 512, 256, 128)),
    }


def kernel(x, c, ada_w, ada_b, norm_mix_g, w_in, b_gates, conv_w, conv_b, gmlp_norm_g, gmlp_w_spatial, gmlp_b_spatial, mlstm_norm_g, w_out, norm_ffn_g, peer_w_query, peer_sub_keys_1, peer_sub_keys_2, peer_expert_down, peer_expert_up, final_norm_g):
    bsz, seq, d = x.shape
    assert bsz == 1 and d == D_MODEL and seq % CHUNK == 0
    assert ada_w.shape[0] == 1
    t = bsz * seq
    xt = x.reshape(t, d)
    c8 = jnp.broadcast_to(c, (SUBLANES, d))
    pad = LANES - GATE_COLS
    tiles = _tiles(t)

    for l in range(1):
        mod = _ada(c8, ada_w[l], ada_b[l][None, :])[0:1]
        shift_m, scale_m, gate_m, shift_f, scale_f, gate_f = jnp.split(mod, N_MOD, axis=-1)

        w_bf = w_in[l][:, :PROJ_COLS].astype(BF16)
        w_gate = jnp.pad(w_in[l][:, PROJ_COLS:], ((0, 0), (0, pad))).astype(BF16)
        b_gate = jnp.pad(b_gates[l], (0, pad))[None, :]
        proj, gates = _inproj(xt, norm_mix_g[l][None, :], scale_m, shift_m, w_bf, w_gate, b_gate,
                              **tiles["inproj"])

        yg = _gmlp(proj, gmlp_w_spatial[l], gmlp_b_spatial[l].T, gmlp_norm_g[l], **tiles["gmlp"])
        ym = _mlstm(proj, gates, conv_w[l], conv_b[l][None, :], mlstm_norm_g[l].reshape(1, D_MLSTM),
                    **tiles["mlstm"])

        x1, h2, s1t, s2t = _outproj(yg, ym, xt, w_out[l].astype(BF16), gate_m,
                                    norm_ffn_g[l][None, :], scale_f, shift_f,
                                    peer_w_query[l].astype(BF16), peer_sub_keys_1[l],
                                    peer_sub_keys_2[l], **tiles["outproj"])
        thrt, e1t, e2t = _route(s1t, s2t, **tiles["route"])
        fg = final_norm_g[None, :]
        xt = _peer(h2, peer_expert_down[l].astype(BF16).T, peer_expert_up[l].astype(BF16),
                   s2t, e2t, thrt, e1t, x1, gate_f, fg, **tiles["peer"])
    return xt.reshape(bsz, seq, d)
```

```python
import functools
import math

import jax
import jax.numpy as jnp
from jax import lax
from jax.experimental import pallas as pl
from jax.experimental.pallas import tpu as pltpu

D_MODEL = 2048
D_GMLP = 1024
GMLP_GROUPS = 8
GMLP_GROUP_DIM = 128
CHUNK = 128
D_MLSTM = 1024
MLSTM_HEADS = 4
MLSTM_HEAD_DIM = 256
CONV_WIDTH = 4
PEER_HEADS = 8
PEER_N_KEYS = 128
PEER_N_EXPERTS = PEER_N_KEYS * PEER_N_KEYS
PEER_HALF_DIM = 128
PEER_TOPK = 16
N_MOD = 6
EPS = 1e-6
PROJ_COLS = 2 * D_GMLP + 4 * D_MLSTM
GATE_COLS = 2 * MLSTM_HEADS
LANES = 128
SUBLANES = 8
VMEM_LIMIT = 56 * 1024 * 1024
PEER_VMEM_LIMIT = 62 * 1024 * 1024

F32 = jnp.float32
BF16 = jnp.bfloat16
NEG_INF = float("-inf")
POS_INF = float("inf")
HIGHEST = lax.Precision.HIGHEST


def _params(*sem):
    return pltpu.CompilerParams(dimension_semantics=sem, vmem_limit_bytes=VMEM_LIMIT)


def _gelu(x):
    return 0.5 * x * (1.0 + lax.erf(x * (1.0 / math.sqrt(2.0))))


def _sigmoid(x):
    return 1.0 / (1.0 + jnp.exp(-x))


def _rms(x):
    return x * lax.rsqrt(jnp.mean(x * x, axis=-1, keepdims=True) + EPS)


def _dot(a, b, **kw):
    return jnp.dot(a, b, preferred_element_type=F32, **kw)


def _dot_nt(a, b, **kw):
    return lax.dot_general(a, b, (((1,), (1,)), ((), ())), preferred_element_type=F32, **kw)


def _dot_tn(a, b, **kw):
    return lax.dot_general(a, b, (((0,), (0,)), ((), ())), preferred_element_type=F32, **kw)


def _ada_kernel(c_ref, w_ref, b_ref, o_ref):
    c = c_ref[...]
    sc = c * _sigmoid(c)
    o_ref[...] = _dot(sc, w_ref[...], precision=HIGHEST) + b_ref[...]


def _ada(c8, ada_w, ada_b):
    d, n = ada_w.shape
    tn = 1024
    return pl.pallas_call(
        _ada_kernel,
        grid=(n // tn,),
        in_specs=[pl.BlockSpec((SUBLANES, d), lambda j: (0, 0)),
                  pl.BlockSpec((d, tn), lambda j: (0, j)),
                  pl.BlockSpec((1, tn), lambda j: (0, j))],
        out_specs=pl.BlockSpec((SUBLANES, tn), lambda j: (0, j)),
        out_shape=jax.ShapeDtypeStruct((SUBLANES, n), F32),
        compiler_params=_params("arbitrary"),
        name="ada",
    )(c8, ada_w, ada_b)


def _inproj_kernel(x_ref, g_ref, sc_ref, sh_ref, w_ref, wg_ref, bg_ref, proj_ref, gate_ref, h_ref):
    @pl.when(pl.program_id(1) == 0)
    def _():
        h = _rms(x_ref[...]) * g_ref[...] * (1.0 + sc_ref[...]) + sh_ref[...]
        h_ref[...] = h.astype(BF16)
        gate_ref[...] = _dot(h_ref[...], wg_ref[...]) + bg_ref[...]

    proj_ref[...] = _dot(h_ref[...], w_ref[...])


def _inproj(x, g, scale, shift, w_bf, w_gate, b_gate, tm, tn):
    t, d = x.shape
    n = PROJ_COLS
    row = lambda i, j: (0, 0)
    return pl.pallas_call(
        _inproj_kernel,
        grid=(t // tm, n // tn),
        in_specs=[pl.BlockSpec((tm, d), lambda i, j: (i, 0)),
                  pl.BlockSpec((1, d), row), pl.BlockSpec((1, d), row), pl.BlockSpec((1, d), row),
                  pl.BlockSpec((d, tn), lambda i, j: (0, j)),
                  pl.BlockSpec((d, LANES), row), pl.BlockSpec((1, LANES), row)],
        out_specs=[pl.BlockSpec((tm, tn), lambda i, j: (i, j)),
                   pl.BlockSpec((tm, LANES), lambda i, j: (i, 0))],
        out_shape=[jax.ShapeDtypeStruct((t, n), F32), jax.ShapeDtypeStruct((t, LANES), F32)],
        scratch_shapes=[pltpu.VMEM((tm, d), BF16)],
        compiler_params=_params("parallel", "arbitrary"),
        name="inproj",
    )(x, g, scale, shift, w_bf, w_gate, b_gate)


def _gmlp_kernel(u_ref, v_ref, w_ref, bt_ref, ng_ref, o_ref, *, chunks):
    row = lax.broadcasted_iota(jnp.int32, (CHUNK, CHUNK), 0)
    col = lax.broadcasted_iota(jnp.int32, (CHUNK, CHUNK), 1)
    causal = row >= col
    for g in range(GMLP_GROUPS):
        w = jnp.where(causal, w_ref[g], 0.0).astype(BF16)
        bias = bt_ref[:, g:g + 1]
        ng = ng_ref[g:g + 1, :]
        cs = slice(g * GMLP_GROUP_DIM, (g + 1) * GMLP_GROUP_DIM)
        for n in range(chunks):
            rs = slice(n * CHUNK, (n + 1) * CHUNK)
            vn = _rms(_gelu(v_ref[rs, cs])) * ng
            mixed = _dot(w, vn.astype(BF16)) + bias
            o_ref[rs, cs] = (_gelu(u_ref[rs, cs]) * mixed).astype(BF16)


def _gmlp(proj, w_spatial, b_spatial_t, norm_g, tg):
    t = proj.shape[0]
    full = lambda *s: pl.BlockSpec(s, lambda i: (0,) * len(s))
    return pl.pallas_call(
        functools.partial(_gmlp_kernel, chunks=tg // CHUNK),
        grid=(t // tg,),
        in_specs=[pl.BlockSpec((tg, D_GMLP), lambda i: (i, 0)),
                  pl.BlockSpec((tg, D_GMLP), lambda i: (i, 1)),
                  full(GMLP_GROUPS, CHUNK, CHUNK), full(CHUNK, GMLP_GROUPS),
                  full(GMLP_GROUPS, GMLP_GROUP_DIM)],
        out_specs=pl.BlockSpec((tg, D_GMLP), lambda i: (i, 0)),
        out_shape=jax.ShapeDtypeStruct((t, D_GMLP), BF16),
        compiler_params=_params("parallel"),
        name="gmlp",
    )(proj, proj, w_spatial, b_spatial_t, norm_g)


def _mlstm_kernel(qk_ref, v_ref, o_ref, gate_ref, cw_ref, cb_ref, ng_ref, out_ref,
                  xext_ref, c_ref, n_ref, m_ref, *, chunks):
    @pl.when(pl.program_id(0) == 0)
    def _():
        xext_ref[0:SUBLANES, :] = jnp.zeros((SUBLANES, 2 * D_MLSTM), F32)
        c_ref[...] = jnp.zeros_like(c_ref)
        n_ref[...] = jnp.zeros_like(n_ref)
        m_ref[...] = jnp.zeros_like(m_ref)

    def chunk(r, carry):
        rs = pl.ds(pl.multiple_of(r * CHUNK, CHUNK), CHUNK)
        _mlstm_chunk(qk_ref.at[rs], v_ref.at[rs], o_ref.at[rs], gate_ref.at[rs], cw_ref, cb_ref,
                     ng_ref, out_ref.at[rs], xext_ref, c_ref, n_ref, m_ref)
        return carry

    lax.fori_loop(0, chunks, chunk, 0)


def _mlstm_chunk(qk_ref, v_ref, o_ref, gate_ref, cw_ref, cb_ref, ng_ref, out_ref,
                 xext_ref, c_ref, n_ref, m_ref):
    L = CHUNK
    dh = MLSTM_HEAD_DIM

    xext_ref[SUBLANES:SUBLANES + L, :] = qk_ref[...]
    conv = cb_ref[...]
    for k in range(CONV_WIDTH):
        off = SUBLANES - (CONV_WIDTH - 1) + k
        conv = conv + cw_ref[k:k + 1, :] * xext_ref[off:off + L, :]
    xext_ref[0:SUBLANES, :] = xext_ref[L:L + SUBLANES, :]
    qk = conv * _sigmoid(conv)

    gates = gate_ref[...]
    log_f = jnp.minimum(gates, 0.0) - jnp.log(1.0 + jnp.exp(-jnp.abs(gates)))
    row = lax.broadcasted_iota(jnp.int32, (L, L), 0)
    col = lax.broadcasted_iota(jnp.int32, (L, L), 1)
    causal = row >= col
    tri = jnp.where(causal, 1.0, 0.0).astype(F32)
    cum = _dot(tri, log_f, precision=HIGHEST)
    lane = lax.broadcasted_iota(jnp.int32, (L, LANES), 1)
    mat = jnp.where(lane < MLSTM_HEADS, gates, cum)
    mat_t = mat.T

    for h in range(MLSTM_HEADS):
        hs = slice(h * dh, (h + 1) * dh)
        q = (qk[:, h * dh:(h + 1) * dh] * (dh ** -0.5)).astype(BF16)
        k = qk[:, D_MLSTM + h * dh:D_MLSTM + (h + 1) * dh]
        v = v_ref[:, hs].astype(BF16)
        i_col = mat[:, h:h + 1]
        b_col = mat[:, MLSTM_HEADS + h:MLSTM_HEADS + h + 1]
        i_row = mat_t[h:h + 1, :]
        b_row = mat_t[MLSTM_HEADS + h:MLSTM_HEADS + h + 1, :]
        m_prev = m_ref[h][0:1, 0:1]
        c_prev = c_ref[h]
        n_prev = n_ref[h][0:1, :]

        log_d = jnp.where(causal, b_col - b_row + i_row, NEG_INF)
        a = b_col + m_prev
        m_comb = jnp.maximum(a, jnp.max(log_d, axis=-1, keepdims=True))
        w_intra = jnp.exp(log_d - m_comb)
        w_inter = jnp.exp(a - m_comb)
        s = _dot_nt(q, k.astype(BF16)) * w_intra
        num = _dot(s.astype(BF16), v) + w_inter * _dot(q, c_prev.astype(BF16))
        qf = q.astype(F32)
        den = (jnp.sum(s, axis=-1, keepdims=True)
               + w_inter * jnp.sum(qf * n_prev, axis=-1, keepdims=True))
        hout = num / jnp.maximum(jnp.abs(den), jnp.exp(-m_comb))

        b_last = b_col[L - 1:L, :]
        log_w = b_last - b_col + i_col
        m_new = jnp.maximum(b_last + m_prev, jnp.max(log_w, axis=0, keepdims=True))
        w_state = jnp.exp(log_w - m_new)
        decay = jnp.exp(b_last + m_prev - m_new)
        kw = w_state * k
        c_ref[h] = decay * c_prev + _dot_tn(kw.astype(BF16), v)
        n_ref[h] = jnp.broadcast_to(decay * n_prev + jnp.sum(kw, axis=0, keepdims=True),
                                    (SUBLANES, dh))
        m_ref[h] = jnp.broadcast_to(m_new, (SUBLANES, LANES))

        y = _rms(hout) * ng_ref[:, hs] * _sigmoid(o_ref[:, hs])
        out_ref[:, hs] = y.astype(BF16)


def _mlstm(proj, gates, conv_w, conv_b, norm_g, tl):
    t = proj.shape[0]
    L = CHUNK
    full = lambda *s: pl.BlockSpec(s, lambda i: (0,) * len(s))
    return pl.pallas_call(
        functools.partial(_mlstm_kernel, chunks=tl // L),
        grid=(t // tl,),
        in_specs=[pl.BlockSpec((tl, 2 * D_MLSTM), lambda i: (i, 1)),
                  pl.BlockSpec((tl, D_MLSTM), lambda i: (i, 4)),
                  pl.BlockSpec((tl, D_MLSTM), lambda i: (i, 5)),
                  pl.BlockSpec((tl, LANES), lambda i: (i, 0)),
                  full(CONV_WIDTH, 2 * D_MLSTM), full(1, 2 * D_MLSTM), full(1, D_MLSTM)],
        out_specs=pl.BlockSpec((tl, D_MLSTM), lambda i: (i, 0)),
        out_shape=jax.ShapeDtypeStruct((t, D_MLSTM), BF16),
        scratch_shapes=[pltpu.VMEM((L + SUBLANES, 2 * D_MLSTM), F32),
                        pltpu.VMEM((MLSTM_HEADS, MLSTM_HEAD_DIM, MLSTM_HEAD_DIM), F32),
                        pltpu.VMEM((MLSTM_HEADS, SUBLANES, MLSTM_HEAD_DIM), F32),
                        pltpu.VMEM((MLSTM_HEADS, SUBLANES, LANES), F32)],
        compiler_params=_params("arbitrary"),
        name="mlstm",
    )(proj, proj, proj, gates, conv_w, conv_b, norm_g)


def _outproj_kernel(yg_ref, ym_ref, x_ref, wo_ref, gm_ref, g_ref, sc_ref, sh_ref, wq_ref,
                    k1_ref, k2_ref, x1_ref, h2_ref, s1_ref, s2_ref):
    mix = _dot(yg_ref[...], wo_ref[0:D_GMLP, :]) + _dot(ym_ref[...], wo_ref[D_GMLP:, :])
    x1 = x_ref[...] + gm_ref[...] * mix
    x1_ref[...] = x1
    h2 = (_rms(x1) * g_ref[...] * (1.0 + sc_ref[...]) + sh_ref[...]).astype(BF16)
    h2_ref[...] = h2
    q = _dot(h2, wq_ref[...]).astype(BF16)
    k1 = k1_ref[...].astype(BF16)
    k2 = k2_ref[...].astype(BF16)
    for h in range(PEER_HEADS):
        base = h * 2 * PEER_HALF_DIM
        s1_ref[h] = _dot_nt(k1, q[:, base:base + PEER_HALF_DIM])
        s2_ref[h] = _dot_nt(k2, q[:, base + PEER_HALF_DIM:base + 2 * PEER_HALF_DIM])


def _outproj(yg, ym, x, w_out, gate_m, g, scale, shift, w_query, keys1, keys2, tm):
    t, d = x.shape
    row = lambda i: (0, 0)
    tok = lambda w: pl.BlockSpec((tm, w), lambda i: (i, 0))
    score = pl.BlockSpec((PEER_HEADS, PEER_N_KEYS, tm), lambda i: (0, 0, i))
    return pl.pallas_call(
        _outproj_kernel,
        grid=(t // tm,),
        in_specs=[tok(D_GMLP), tok(D_MLSTM), tok(d),
                  pl.BlockSpec((d, d), row, pipeline_mode=pl.Buffered(1)),
                  pl.BlockSpec((1, d), row), pl.BlockSpec((1, d), row),
                  pl.BlockSpec((1, d), row), pl.BlockSpec((1, d), row),
                  pl.BlockSpec((d, d), row, pipeline_mode=pl.Buffered(1)),
                  pl.BlockSpec((PEER_N_KEYS, PEER_HALF_DIM), row),
                  pl.BlockSpec((PEER_N_KEYS, PEER_HALF_DIM), row)],
        out_specs=[tok(d), tok(d), score, score],
        out_shape=[jax.ShapeDtypeStruct((t, d), F32), jax.ShapeDtypeStruct((t, d), BF16),
                   jax.ShapeDtypeStruct((PEER_HEADS, PEER_N_KEYS, t), F32),
                   jax.ShapeDtypeStruct((PEER_HEADS, PEER_N_KEYS, t), F32)],
        compiler_params=_params("parallel"),
        name="outproj",
    )(yg, ym, x, w_out, gate_m, g, scale, shift, w_query, keys1, keys2)


def _pair_candidates():
    return [(a, b) for a in range(PEER_TOPK) for b in range(PEER_TOPK)
            if (a + 1) * (b + 1) <= PEER_TOPK]


def _sort_network(n):
    def merge(lo, hi, r):
        step = r * 2
        if step < hi - lo:
            yield from merge(lo, hi, step)
            yield from merge(lo + r, hi, step)
            yield from [(i, i + r) for i in range(lo + r, hi - r, step)]
        else:
            yield (lo, lo + r)

    def sort(lo, hi):
        if hi - lo >= 1:
            mid = lo + (hi - lo) // 2
            yield from sort(lo, mid)
            yield from sort(mid + 1, hi)
            yield from merge(lo, hi, 1)

    return list(sort(0, n - 1))


def _compare_exchange(v, i, j):
    v[i], v[j] = jnp.maximum(v[i], v[j]), jnp.minimum(v[i], v[j])


def _bitonic_sort(v):
    d = PEER_TOPK // 2
    while d >= 1:
        for i in range(PEER_TOPK):
            if i & d == 0:
                _compare_exchange(v, i, i + d)
        d //= 2


def _top16(tiles):
    v = list(tiles)
    for i, j in _sort_network(PEER_TOPK):
        _compare_exchange(v, i, j)
    for shift in (4, 2, 1):
        other = [pltpu.roll(x, shift, axis=0) for x in v]
        v = [jnp.maximum(v[i], other[PEER_TOPK - 1 - i]) for i in range(PEER_TOPK)]
        _bitonic_sort(v)
    return v


def _route_kernel(s1_ref, s2_ref, thr_ref, e1_ref, e2_ref, *, cols):
    groups = PEER_N_KEYS // SUBLANES
    sub = lax.broadcasted_iota(jnp.int32, (SUBLANES, LANES), 0)
    pairs = _pair_candidates()

    def rows(g):
        return pl.ds(g * SUBLANES, SUBLANES)

    def column(c, carry):
        tok = pl.ds(pl.multiple_of(c * LANES, LANES), LANES)
        v1s = [jnp.zeros((SUBLANES, LANES), F32)] * PEER_TOPK
        v2s = [jnp.zeros((SUBLANES, LANES), F32)] * PEER_TOPK
        for h in range(PEER_HEADS):
            t1 = _top16([s1_ref[h, rows(g), tok] for g in range(groups)])
            t2 = _top16([s2_ref[h, rows(g), tok] for g in range(groups)])
            v1s = [jnp.where(sub == h, t1[a], v1s[a]) for a in range(PEER_TOPK)]
            v2s = [jnp.where(sub == h, t2[a], v2s[a]) for a in range(PEER_TOPK)]

        cands = [v1s[a] + v2s[b] for a, b in pairs]
        row = {ab: cnd for ab, cnd in zip(pairs, cands)}
        top = [row[(0, b)] for b in range(PEER_TOPK)]
        for a in range(1, PEER_TOPK):
            for b in range(PEER_TOPK // (a + 1)):
                i = PEER_TOPK - 1 - b
                top[i] = jnp.maximum(top[i], row[(a, b)])
            _bitonic_sort(top)
        tau = top[PEER_TOPK - 1]

        e2s = [jnp.exp(v2s[b] - v2s[0]) for b in range(PEER_TOPK)]
        thr_a = [jnp.full((SUBLANES, LANES), POS_INF, F32)] * PEER_TOPK
        zs_a = [jnp.zeros((SUBLANES, LANES), F32)] * PEER_TOPK
        for (a, b), cnd in zip(pairs, cands):
            hit = cnd >= tau
            thr_a[a] = jnp.where(hit, v2s[b], thr_a[a])
            zs_a[a] = zs_a[a] + jnp.where(hit, e2s[b], 0.0)
        z = jnp.zeros((SUBLANES, LANES), F32)
        for a in range(PEER_TOPK):
            z = z + jnp.exp(v1s[a] - v1s[0]) * zs_a[a]
        inv_z = 1.0 / z

        for h in range(PEER_HEADS):
            def rep(x):
                return jnp.broadcast_to(x[h:h + 1, :], (SUBLANES, LANES))
            v1r = [rep(v1s[a]) for a in range(PEER_TOPK)]
            thr_r = [rep(thr_a[a]) for a in range(PEER_TOPK)]
            max2 = rep(v2s[0])
            inv_zr = rep(inv_z)
            for g in range(groups):
                s1 = s1_ref[h, rows(g), tok]
                thr = jnp.full((SUBLANES, LANES), POS_INF, F32)
                for a in range(PEER_TOPK):
                    thr = jnp.where(s1 == v1r[a], thr_r[a], thr)
                thr_ref[h, rows(g), tok] = thr
                e1_ref[h, rows(g), tok] = jnp.exp(s1 - v1r[0]) * inv_zr
                e2_ref[h, rows(g), tok] = jnp.exp(s2_ref[h, rows(g), tok] - max2)
        return carry

    lax.fori_loop(0, cols, column, 0)


def _route(s1t, s2t, tc):
    t = s1t.shape[2]
    spec = pl.BlockSpec((PEER_HEADS, PEER_N_KEYS, tc), lambda i: (0, 0, i))
    shape = jax.ShapeDtypeStruct(s1t.shape, F32)
    return pl.pallas_call(
        functools.partial(_route_kernel, cols=tc // LANES),
        grid=(t // tc,),
        in_specs=[spec, spec],
        out_specs=[spec, spec, spec],
        out_shape=[shape, shape, shape],
        compiler_params=_params("parallel"),
        name="route",
    )(s1t, s2t)


def _peer_kernel(h2_ref, down_ref, up_ref, s2_ref, e2_ref, thr_ref, e1_ref, x1_ref, gf_ref, fg_ref,
                 out_ref, a0_ref, a1_ref, w0_ref, w1_ref, *, tb, nj, steps):
    n = pl.program_id(0)
    j_b = jnp.clip(n - 1, 0, steps - 1) % nj
    j_c = jnp.clip(n - 2, 0, steps - 1) % nj
    groups = PEER_N_KEYS // SUBLANES

    @pl.when(n == 0)
    def _():
        a1_ref[...] = jnp.zeros_like(a1_ref)
        w0_ref[...] = jnp.zeros_like(w0_ref)
        w1_ref[...] = jnp.zeros_like(w1_ref)

    @pl.when(j_c == 0)
    def _():
        out_ref[...] = jnp.zeros_like(out_ref)

    rows = pl.ds(pl.multiple_of(j_b * SUBLANES, SUBLANES), SUBLANES)

    def column(c, a_ref, w_ref):
        tok = pl.ds(c * LANES, LANES)
        thr8 = [thr_ref[h, rows, tok] for h in range(PEER_HEADS)]
        e18 = [e1_ref[h, rows, tok] for h in range(PEER_HEADS)]
        for s in range(SUBLANES):
            acc = [jnp.zeros((SUBLANES, LANES), F32) for _ in range(groups)]
            for h in range(PEER_HEADS):
                thr = jnp.broadcast_to(thr8[h][s:s + 1, :], (SUBLANES, LANES))
                e1 = jnp.broadcast_to(e18[h][s:s + 1, :], (SUBLANES, LANES))
                for g in range(groups):
                    ks = pl.ds(g * SUBLANES, SUBLANES)
                    sel = jnp.where(s2_ref[h, ks, tok] >= thr, e2_ref[h, ks, tok], 0.0)
                    acc[g] = acc[g] + sel * e1
            gt = jnp.concatenate(acc, axis=0)
            es = slice(s * PEER_N_KEYS, (s + 1) * PEER_N_KEYS)
            w_ref[tok, es] = (_gelu(a_ref[tok, es]) * gt.T).astype(BF16)

    def stages(a_write, a_read, w_write, w_read):
        a_write[...] = _dot(h2_ref[...], down_ref[...])
        for c in range(tb // LANES):
            column(c, a_read, w_write)
        out_ref[...] += _dot(w_read[...], up_ref[...])

    @pl.when(n % 2 == 0)
    def _():
        stages(a0_ref, a1_ref, w1_ref, w0_ref)

    @pl.when(n % 2 == 1)
    def _():
        stages(a1_ref, a0_ref, w0_ref, w1_ref)

    @pl.when(j_c == nj - 1)
    def _():
        x2 = x1_ref[...] + gf_ref[...] * out_ref[...]
        out_ref[...] = _rms(x2) * fg_ref[...]


def _peer(h2, down_t, up_bf, s2t, e2t, thrt, e1t, x1, gate_f, final_g, tb):
    t, d = x1.shape
    se = SUBLANES * PEER_N_KEYS
    nj = PEER_N_EXPERTS // se
    steps = (t // tb) * nj
    stage_a = lambda n: jnp.minimum(n, steps - 1)
    stage_b = lambda n: jnp.clip(n - 1, 0, steps - 1)
    stage_c = lambda n: jnp.clip(n - 2, 0, steps - 1)
    row = lambda n: (0, 0)
    route = lambda: pl.BlockSpec((PEER_HEADS, PEER_N_KEYS, tb), lambda n: (0, 0, stage_b(n) // nj))
    return pl.pallas_call(
        functools.partial(_peer_kernel, tb=tb, nj=nj, steps=steps),
        grid=(steps + 2,),
        in_specs=[pl.BlockSpec((tb, d), lambda n: (stage_a(n) // nj, 0)),
                  pl.BlockSpec((d, se), lambda n: (0, stage_a(n) % nj)),
                  pl.BlockSpec((se, d), lambda n: (stage_c(n) % nj, 0)),
                  route(), route(), route(), route(),
                  pl.BlockSpec((tb, d), lambda n: (stage_c(n) // nj, 0),
                               pipeline_mode=pl.Buffered(1)),
                  pl.BlockSpec((1, d), row), pl.BlockSpec((1, d), row)],
        out_specs=pl.BlockSpec((tb, d), lambda n: (stage_c(n) // nj, 0)),
        out_shape=jax.ShapeDtypeStruct((t, d), F32),
        scratch_shapes=[pltpu.VMEM((tb, se), F32), pltpu.VMEM((tb, se), F32),
                        pltpu.VMEM((tb, se), BF16), pltpu.VMEM((tb, se), BF16)],
        compiler_params=pltpu.CompilerParams(dimension_semantics=("arbitrary",),
                                             vmem_limit_bytes=PEER_VMEM_LIMIT),
        name="peer",
    )(h2, down_t, up_bf, s2t, e2t, thrt, e1t, x1, gate_f, final_g)


def _pick(n, *cands):
    for cnd in cands:
        if n % cnd == 0:
            return cnd
    raise ValueError(f"unsupported token count {n}")


def _tiles(t):
    return {
        "inproj": dict(tm=_pick(t, 1024, 512, 256, 128), tn=1536),
        "gmlp": dict(tg=_pick(t, 512, 256, 128)),
        "mlstm": dict(tl=_pick(t, 512, 256, 128)),
        "outproj": dict(tm=_pick(t, 256, 128)),
        "route": dict(tc=_pick(t, 512, 256, 128)),
        "peer": dict(tb=_pick(t, 512, 256, 128)),
    }


def kernel(x, c, ada_w, ada_b, norm_mix_g, w_in, b_gates, conv_w, conv_b, gmlp_norm_g, gmlp_w_spatial, gmlp_b_spatial, mlstm_norm_g, w_out, norm_ffn_g, peer_w_query, peer_sub_keys_1, peer_sub_keys_2, peer_expert_down, peer_expert_up, final_norm_g):
    bsz, seq, d = x.shape
    assert bsz == 1 and d == D_MODEL and seq % CHUNK == 0
    assert ada_w.shape[0] == 1
    t = bsz * seq
    xt = x.reshape(t, d)
    c8 = jnp.broadcast_to(c, (SUBLANES, d))
    pad = LANES - GATE_COLS
    tiles = _tiles(t)

    for l in range(1):
        mod = _ada(c8, ada_w[l], ada_b[l][None, :])[0:1]
        shift_m, scale_m, gate_m, shift_f, scale_f, gate_f = jnp.split(mod, N_MOD, axis=-1)

        w_bf = w_in[l].astype(BF16)
        w_gate = jnp.pad(w_bf[:, PROJ_COLS:], ((0, 0), (0, pad)))
        b_gate = jnp.pad(b_gates[l], (0, pad))[None, :]
        proj, gates = _inproj(xt, norm_mix_g[l][None, :], scale_m, shift_m, w_bf, w_gate, b_gate,
                              **tiles["inproj"])

        yg = _gmlp(proj, gmlp_w_spatial[l], gmlp_b_spatial[l].T, gmlp_norm_g[l], **tiles["gmlp"])
        ym = _mlstm(proj, gates, conv_w[l], conv_b[l][None, :], mlstm_norm_g[l].reshape(1, D_MLSTM),
                    **tiles["mlstm"])

        x1, h2, s1t, s2t = _outproj(yg, ym, xt, w_out[l].astype(BF16), gate_m,
                                    norm_ffn_g[l][None, :], scale_f, shift_f,
                                    peer_w_query[l].astype(BF16), peer_sub_keys_1[l],
                                    peer_sub_keys_2[l], **tiles["outproj"])
        thrt, e1t, e2t = _route(s1t, s2t, **tiles["route"])
        fg = final_norm_g[None, :]
        xt = _peer(h2, peer_expert_down[l].astype(BF16).T, peer_expert_up[l].astype(BF16),
                   s2t, e2t, thrt, e1t, x1, gate_f, fg, **tiles["peer"])
    return xt.reshape(bsz, seq, d)
```

```python
import functools
import math

import jax
import jax.numpy as jnp
from jax import lax
from jax.experimental import pallas as pl
from jax.experimental.pallas import tpu as pltpu

D_MODEL = 2048
D_GMLP = 1024
GMLP_GROUPS = 8
GMLP_GROUP_DIM = 128
CHUNK = 128
D_MLSTM = 1024
MLSTM_HEADS = 4
MLSTM_HEAD_DIM = 256
CONV_WIDTH = 4
PEER_HEADS = 8
PEER_N_KEYS = 128
PEER_N_EXPERTS = PEER_N_KEYS * PEER_N_KEYS
PEER_HALF_DIM = 128
PEER_TOPK = 16
N_MOD = 6
EPS = 1e-6
PROJ_COLS = 2 * D_GMLP + 4 * D_MLSTM
GATE_COLS = 2 * MLSTM_HEADS
LANES = 128
SUBLANES = 8
VMEM_LIMIT = 56 * 1024 * 1024
PEER_VMEM_LIMIT = 62 * 1024 * 1024

F32 = jnp.float32
BF16 = jnp.bfloat16
NEG_INF = float("-inf")
POS_INF = float("inf")
HIGHEST = lax.Precision.HIGHEST


def _params(*sem):
    return pltpu.CompilerParams(dimension_semantics=sem, vmem_limit_bytes=VMEM_LIMIT)


def _gelu(x):
    return 0.5 * x * (1.0 + lax.erf(x * (1.0 / math.sqrt(2.0))))


def _sigmoid(x):
    return 1.0 / (1.0 + jnp.exp(-x))


def _rms(x):
    return x * lax.rsqrt(jnp.mean(x * x, axis=-1, keepdims=True) + EPS)


def _dot(a, b, **kw):
    return jnp.dot(a, b, preferred_element_type=F32, **kw)


def _dot_nt(a, b, **kw):
    return lax.dot_general(a, b, (((1,), (1,)), ((), ())), preferred_element_type=F32, **kw)


def _dot_tn(a, b, **kw):
    return lax.dot_general(a, b, (((0,), (0,)), ((), ())), preferred_element_type=F32, **kw)


def _ada_kernel(c_ref, w_ref, b_ref, o_ref):
    c = c_ref[...]
    sc = c * _sigmoid(c)
    o_ref[...] = _dot(sc, w_ref[...], precision=HIGHEST) + b_ref[...]


def _ada(c8, ada_w, ada_b):
    d, n = ada_w.shape
    tn = 1024
    return pl.pallas_call(
        _ada_kernel,
        grid=(n // tn,),
        in_specs=[pl.BlockSpec((SUBLANES, d), lambda j: (0, 0)),
                  pl.BlockSpec((d, tn), lambda j: (0, j)),
                  pl.BlockSpec((1, tn), lambda j: (0, j))],
        out_specs=pl.BlockSpec((SUBLANES, tn), lambda j: (0, j)),
        out_shape=jax.ShapeDtypeStruct((SUBLANES, n), F32),
        compiler_params=_params("arbitrary"),
        name="ada",
    )(c8, ada_w, ada_b)


def _inproj_kernel(x_ref, g_ref, sc_ref, sh_ref, w_ref, wg_ref, bg_ref, proj_ref, gate_ref, h_ref):
    @pl.when(pl.program_id(1) == 0)
    def _():
        h = _rms(x_ref[...]) * g_ref[...] * (1.0 + sc_ref[...]) + sh_ref[...]
        h_ref[...] = h.astype(BF16)
        gate_ref[...] = _dot(h_ref[...], wg_ref[...]) + bg_ref[...]

    proj_ref[...] = _dot(h_ref[...], w_ref[...])


def _inproj(x, g, scale, shift, w_bf, w_gate, b_gate, tm, tn):
    t, d = x.shape
    n = PROJ_COLS
    row = lambda i, j: (0, 0)
    return pl.pallas_call(
        _inproj_kernel,
        grid=(t // tm, n // tn),
        in_specs=[pl.BlockSpec((tm, d), lambda i, j: (i, 0)),
                  pl.BlockSpec((1, d), row), pl.BlockSpec((1, d), row), pl.BlockSpec((1, d), row),
                  pl.BlockSpec((d, tn), lambda i, j: (0, j)),
                  pl.BlockSpec((d, LANES), row), pl.BlockSpec((1, LANES), row)],
        out_specs=[pl.BlockSpec((tm, tn), lambda i, j: (i, j)),
                   pl.BlockSpec((tm, LANES), lambda i, j: (i, 0))],
        out_shape=[jax.ShapeDtypeStruct((t, n), F32), jax.ShapeDtypeStruct((t, LANES), F32)],
        scratch_shapes=[pltpu.VMEM((tm, d), BF16)],
        compiler_params=_params("parallel", "arbitrary"),
        name="inproj",
    )(x, g, scale, shift, w_bf, w_gate, b_gate)


def _gmlp_kernel(u_ref, v_ref, w_ref, bt_ref, ng_ref, o_ref, *, chunks):
    row = lax.broadcasted_iota(jnp.int32, (CHUNK, CHUNK), 0)
    col = lax.broadcasted_iota(jnp.int32, (CHUNK, CHUNK), 1)
    causal = row >= col
    for g in range(GMLP_GROUPS):
        w = jnp.where(causal, w_ref[g], 0.0).astype(BF16)
        bias = bt_ref[:, g:g + 1]
        ng = ng_ref[g:g + 1, :]
        cs = slice(g * GMLP_GROUP_DIM, (g + 1) * GMLP_GROUP_DIM)
        for n in range(chunks):
            rs = slice(n * CHUNK, (n + 1) * CHUNK)
            vn = _rms(_gelu(v_ref[rs, cs])) * ng
            mixed = _dot(w, vn.astype(BF16)) + bias
            o_ref[rs, cs] = (_gelu(u_ref[rs, cs]) * mixed).astype(BF16)


def _gmlp(proj, w_spatial, b_spatial_t, norm_g, tg):
    t = proj.shape[0]
    full = lambda *s: pl.BlockSpec(s, lambda i: (0,) * len(s))
    return pl.pallas_call(
        functools.partial(_gmlp_kernel, chunks=tg // CHUNK),
        grid=(t // tg,),
        in_specs=[pl.BlockSpec((tg, D_GMLP), lambda i: (i, 0)),
                  pl.BlockSpec((tg, D_GMLP), lambda i: (i, 1)),
                  full(GMLP_GROUPS, CHUNK, CHUNK), full(CHUNK, GMLP_GROUPS),
                  full(GMLP_GROUPS, GMLP_GROUP_DIM)],
        out_specs=pl.BlockSpec((tg, D_GMLP), lambda i: (i, 0)),
        out_shape=jax.ShapeDtypeStruct((t, D_GMLP), BF16),
        compiler_params=_params("parallel"),
        name="gmlp",
    )(proj, proj, w_spatial, b_spatial_t, norm_g)


def _mlstm_kernel(qk_ref, v_ref, o_ref, gate_ref, cw_ref, cb_ref, ng_ref, out_ref,
                  xext_ref, c_ref, n_ref, m_ref, *, chunks):
    @pl.when(pl.program_id(0) == 0)
    def _():
        xext_ref[0:SUBLANES, :] = jnp.zeros((SUBLANES, 2 * D_MLSTM), F32)
        c_ref[...] = jnp.zeros_like(c_ref)
        n_ref[...] = jnp.zeros_like(n_ref)
        m_ref[...] = jnp.zeros_like(m_ref)

    def chunk(r, carry):
        rs = pl.ds(pl.multiple_of(r * CHUNK, CHUNK), CHUNK)
        _mlstm_chunk(qk_ref.at[rs], v_ref.at[rs], o_ref.at[rs], gate_ref.at[rs], cw_ref, cb_ref,
                     ng_ref, out_ref.at[rs], xext_ref, c_ref, n_ref, m_ref)
        return carry

    lax.fori_loop(0, chunks, chunk, 0)


def _mlstm_chunk(qk_ref, v_ref, o_ref, gate_ref, cw_ref, cb_ref, ng_ref, out_ref,
                 xext_ref, c_ref, n_ref, m_ref):
    L = CHUNK
    dh = MLSTM_HEAD_DIM

    xext_ref[SUBLANES:SUBLANES + L, :] = qk_ref[...]
    conv = cb_ref[...]
    for k in range(CONV_WIDTH):
        off = SUBLANES - (CONV_WIDTH - 1) + k
        conv = conv + cw_ref[k:k + 1, :] * xext_ref[off:off + L, :]
    xext_ref[0:SUBLANES, :] = xext_ref[L:L + SUBLANES, :]
    qk = conv * _sigmoid(conv)

    gates = gate_ref[...]
    log_f = jnp.minimum(gates, 0.0) - jnp.log(1.0 + jnp.exp(-jnp.abs(gates)))
    row = lax.broadcasted_iota(jnp.int32, (L, L), 0)
    col = lax.broadcasted_iota(jnp.int32, (L, L), 1)
    causal = row >= col
    tri = jnp.where(causal, 1.0, 0.0).astype(F32)
    cum = _dot(tri, log_f, precision=HIGHEST)
    lane = lax.broadcasted_iota(jnp.int32, (L, LANES), 1)
    mat = jnp.where(lane < MLSTM_HEADS, gates, cum)
    mat_t = mat.T

    for h in range(MLSTM_HEADS):
        hs = slice(h * dh, (h + 1) * dh)
        q = (qk[:, h * dh:(h + 1) * dh] * (dh ** -0.5)).astype(BF16)
        k = qk[:, D_MLSTM + h * dh:D_MLSTM + (h + 1) * dh]
        v = v_ref[:, hs].astype(BF16)
        i_col = mat[:, h:h + 1]
        b_col = mat[:, MLSTM_HEADS + h:MLSTM_HEADS + h + 1]
        i_row = mat_t[h:h + 1, :]
        b_row = mat_t[MLSTM_HEADS + h:MLSTM_HEADS + h + 1, :]
        m_prev = m_ref[h][0:1, 0:1]
        c_prev = c_ref[h]
        n_prev = n_ref[h][0:1, :]

        log_d = jnp.where(causal, b_col - b_row + i_row, NEG_INF)
        a = b_col + m_prev
        m_comb = jnp.maximum(a, jnp.max(log_d, axis=-1, keepdims=True))
        w_intra = jnp.exp(log_d - m_comb)
        w_inter = jnp.exp(a - m_comb)
        s = _dot_nt(q, k.astype(BF16)) * w_intra
        num = _dot(s.astype(BF16), v) + w_inter * _dot(q, c_prev.astype(BF16))
        qf = q.astype(F32)
        den = (jnp.sum(s, axis=-1, keepdims=True)
               + w_inter * jnp.sum(qf * n_prev, axis=-1, keepdims=True))
        hout = num / jnp.maximum(jnp.abs(den), jnp.exp(-m_comb))

        b_last = b_col[L - 1:L, :]
        log_w = b_last - b_col + i_col
        m_new = jnp.maximum(b_last + m_prev, jnp.max(log_w, axis=0, keepdims=True))
        w_state = jnp.exp(log_w - m_new)
        decay = jnp.exp(b_last + m_prev - m_new)
        kw = w_state * k
        c_ref[h] = decay * c_prev + _dot_tn(kw.astype(BF16), v)
        n_ref[h] = jnp.broadcast_to(decay * n_prev + jnp.sum(kw, axis=0, keepdims=True),
                                    (SUBLANES, dh))
        m_ref[h] = jnp.broadcast_to(m_new, (SUBLANES, LANES))

        y = _rms(hout) * ng_ref[:, hs] * _sigmoid(o_ref[:, hs])
        out_ref[:, hs] = y.astype(BF16)


def _mlstm(proj, gates, conv_w, conv_b, norm_g, tl):
    t = proj.shape[0]
    L = CHUNK
    full = lambda *s: pl.BlockSpec(s, lambda i: (0,) * len(s))
    return pl.pallas_call(
        functools.partial(_mlstm_kernel, chunks=tl // L),
        grid=(t // tl,),
        in_specs=[pl.BlockSpec((tl, 2 * D_MLSTM), lambda i: (i, 1)),
                  pl.BlockSpec((tl, D_MLSTM), lambda i: (i, 4)),
                  pl.BlockSpec((tl, D_MLSTM), lambda i: (i, 5)),
                  pl.BlockSpec((tl, LANES), lambda i: (i, 0)),
                  full(CONV_WIDTH, 2 * D_MLSTM), full(1, 2 * D_MLSTM), full(1, D_MLSTM)],
        out_specs=pl.BlockSpec((tl, D_MLSTM), lambda i: (i, 0)),
        out_shape=jax.ShapeDtypeStruct((t, D_MLSTM), BF16),
        scratch_shapes=[pltpu.VMEM((L + SUBLANES, 2 * D_MLSTM), F32),
                        pltpu.VMEM((MLSTM_HEADS, MLSTM_HEAD_DIM, MLSTM_HEAD_DIM), F32),
                        pltpu.VMEM((MLSTM_HEADS, SUBLANES, MLSTM_HEAD_DIM), F32),
                        pltpu.VMEM((MLSTM_HEADS, SUBLANES, LANES), F32)],
        compiler_params=_params("arbitrary"),
        name="mlstm",
    )(proj, proj, proj, gates, conv_w, conv_b, norm_g)


def _outproj_kernel(yg_ref, ym_ref, x_ref, wo_ref, gm_ref, g_ref, sc_ref, sh_ref, wq_ref,
                    k1_ref, k2_ref, x1_ref, h2_ref, s1_ref, s2_ref):
    mix = _dot(yg_ref[...], wo_ref[0:D_GMLP, :]) + _dot(ym_ref[...], wo_ref[D_GMLP:, :])
    x1 = x_ref[...] + gm_ref[...] * mix
    x1_ref[...] = x1
    h2 = (_rms(x1) * g_ref[...] * (1.0 + sc_ref[...]) + sh_ref[...]).astype(BF16)
    h2_ref[...] = h2
    q = _dot(h2, wq_ref[...]).astype(BF16)
    k1 = k1_ref[...].astype(BF16)
    k2 = k2_ref[...].astype(BF16)
    for h in range(PEER_HEADS):
        base = h * 2 * PEER_HALF_DIM
        s1_ref[h] = _dot_nt(k1, q[:, base:base + PEER_HALF_DIM])
        s2_ref[h] = _dot_nt(k2, q[:, base + PEER_HALF_DIM:base + 2 * PEER_HALF_DIM])


def _outproj(yg, ym, x, w_out, gate_m, g, scale, shift, w_query, keys1, keys2, tm):
    t, d = x.shape
    row = lambda i: (0, 0)
    tok = lambda w: pl.BlockSpec((tm, w), lambda i: (i, 0))
    score = pl.BlockSpec((PEER_HEADS, PEER_N_KEYS, tm), lambda i: (0, 0, i))
    return pl.pallas_call(
        _outproj_kernel,
        grid=(t // tm,),
        in_specs=[tok(D_GMLP), tok(D_MLSTM), tok(d),
                  pl.BlockSpec((d, d), row, pipeline_mode=pl.Buffered(1)),
                  pl.BlockSpec((1, d), row), pl.BlockSpec((1, d), row),
                  pl.BlockSpec((1, d), row), pl.BlockSpec((1, d), row),
                  pl.BlockSpec((d, d), row, pipeline_mode=pl.Buffered(1)),
                  pl.BlockSpec((PEER_N_KEYS, PEER_HALF_DIM), row),
                  pl.BlockSpec((PEER_N_KEYS, PEER_HALF_DIM), row)],
        out_specs=[tok(d), tok(d), score, score],
        out_shape=[jax.ShapeDtypeStruct((t, d), F32), jax.ShapeDtypeStruct((t, d), BF16),
                   jax.ShapeDtypeStruct((PEER_HEADS, PEER_N_KEYS, t), F32),
                   jax.ShapeDtypeStruct((PEER_HEADS, PEER_N_KEYS, t), F32)],
        compiler_params=_params("parallel"),
        name="outproj",
    )(yg, ym, x, w_out, gate_m, g, scale, shift, w_query, keys1, keys2)


def _pair_candidates():
    return [(a, b) for a in range(PEER_TOPK) for b in range(PEER_TOPK)
            if (a + 1) * (b + 1) <= PEER_TOPK]


def _sort_network(n):
    def merge(lo, hi, r):
        step = r * 2
        if step < hi - lo:
            yield from merge(lo, hi, step)
            yield from merge(lo + r, hi, step)
            yield from [(i, i + r) for i in range(lo + r, hi - r, step)]
        else:
            yield (lo, lo + r)

    def sort(lo, hi):
        if hi - lo >= 1:
            mid = lo + (hi - lo) // 2
            yield from sort(lo, mid)
            yield from sort(mid + 1, hi)
            yield from merge(lo, hi, 1)

    return list(sort(0, n - 1))


def _compare_exchange(v, i, j):
    v[i], v[j] = jnp.maximum(v[i], v[j]), jnp.minimum(v[i], v[j])


def _bitonic_sort(v):
    d = PEER_TOPK // 2
    while d >= 1:
        for i in range(PEER_TOPK):
            if i & d == 0:
                _compare_exchange(v, i, i + d)
        d //= 2


def _top16(tiles):
    v = list(tiles)
    for i, j in _sort_network(PEER_TOPK):
        _compare_exchange(v, i, j)
    for shift in (4, 2, 1):
        other = [pltpu.roll(x, shift, axis=0) for x in v]
        v = [jnp.maximum(v[i], other[PEER_TOPK - 1 - i]) for i in range(PEER_TOPK)]
        _bitonic_sort(v)
    return v


def _route_kernel(s1_ref, s2_ref, thr_ref, e1_ref, e2_ref, *, cols):
    groups = PEER_N_KEYS // SUBLANES
    sub = lax.broadcasted_iota(jnp.int32, (SUBLANES, LANES), 0)
    pairs = _pair_candidates()

    def rows(g):
        return pl.ds(g * SUBLANES, SUBLANES)

    def column(c, carry):
        tok = pl.ds(pl.multiple_of(c * LANES, LANES), LANES)
        v1s = [jnp.zeros((SUBLANES, LANES), F32)] * PEER_TOPK
        v2s = [jnp.zeros((SUBLANES, LANES), F32)] * PEER_TOPK
        for h in range(PEER_HEADS):
            t1 = _top16([s1_ref[h, rows(g), tok] for g in range(groups)])
            t2 = _top16([s2_ref[h, rows(g), tok] for g in range(groups)])
            v1s = [jnp.where(sub == h, t1[a], v1s[a]) for a in range(PEER_TOPK)]
            v2s = [jnp.where(sub == h, t2[a], v2s[a]) for a in range(PEER_TOPK)]

        cands = [v1s[a] + v2s[b] for a, b in pairs]
        row = {ab: cnd for ab, cnd in zip(pairs, cands)}
        top = [row[(0, b)] for b in range(PEER_TOPK)]
        for a in range(1, PEER_TOPK):
            for b in range(PEER_TOPK // (a + 1)):
                i = PEER_TOPK - 1 - b
                top[i] = jnp.maximum(top[i], row[(a, b)])
            _bitonic_sort(top)
        tau = top[PEER_TOPK - 1]

        e2s = [jnp.exp(v2s[b] - v2s[0]) for b in range(PEER_TOPK)]
        thr_a = [jnp.full((SUBLANES, LANES), POS_INF, F32)] * PEER_TOPK
        zs_a = [jnp.zeros((SUBLANES, LANES), F32)] * PEER_TOPK
        for (a, b), cnd in zip(pairs, cands):
            hit = cnd >= tau
            thr_a[a] = jnp.where(hit, v2s[b], thr_a[a])
            zs_a[a] = zs_a[a] + jnp.where(hit, e2s[b], 0.0)
        z = jnp.zeros((SUBLANES, LANES), F32)
        for a in range(PEER_TOPK):
            z = z + jnp.exp(v1s[a] - v1s[0]) * zs_a[a]
        inv_z = math.sqrt(0.5) / z

        for h in range(PEER_HEADS):
            def rep(x):
                return jnp.broadcast_to(x[h:h + 1, :], (SUBLANES, LANES))
            v1r = [rep(v1s[a]) for a in range(PEER_TOPK)]
            thr_r = [rep(thr_a[a]) for a in range(PEER_TOPK)]
            max2 = rep(v2s[0])
            inv_zr = rep(inv_z)
            for g in range(groups):
                s1 = s1_ref[h, rows(g), tok]
                thr = jnp.full((SUBLANES, LANES), POS_INF, F32)
                for a in range(PEER_TOPK):
                    thr = jnp.where(s1 == v1r[a], thr_r[a], thr)
                thr_ref[h, rows(g), tok] = thr
                e1_ref[h, rows(g), tok] = jnp.exp(s1 - v1r[0]) * inv_zr
                e2_ref[h, rows(g), tok] = jnp.exp(s2_ref[h, rows(g), tok] - max2)
        return carry

    lax.fori_loop(0, cols, column, 0)


def _route(s1t, s2t, tc):
    t = s1t.shape[2]
    spec = pl.BlockSpec((PEER_HEADS, PEER_N_KEYS, tc), lambda i: (0, 0, i))
    shape = jax.ShapeDtypeStruct(s1t.shape, F32)
    return pl.pallas_call(
        functools.partial(_route_kernel, cols=tc // LANES),
        grid=(t // tc,),
        in_specs=[spec, spec],
        out_specs=[spec, spec, spec],
        out_shape=[shape, shape, shape],
        compiler_params=_params("parallel"),
        name="route",
    )(s1t, s2t)


def _peer_kernel(h2_ref, down_ref, up_ref, s2_ref, e2_ref, thr_ref, e1_ref, x1_ref, gf_ref, fg_ref,
                 out_ref, a0_ref, a1_ref, w0_ref, w1_ref, *, tb, nj, steps):
    n = pl.program_id(0)
    j_b = jnp.clip(n - 1, 0, steps - 1) % nj
    j_c = jnp.clip(n - 2, 0, steps - 1) % nj
    groups = PEER_N_KEYS // SUBLANES

    @pl.when(n == 0)
    def _():
        a1_ref[...] = jnp.zeros_like(a1_ref)
        w0_ref[...] = jnp.zeros_like(w0_ref)
        w1_ref[...] = jnp.zeros_like(w1_ref)

    @pl.when(j_c == 0)
    def _():
        out_ref[...] = jnp.zeros_like(out_ref)

    rows = pl.ds(pl.multiple_of(j_b * SUBLANES, SUBLANES), SUBLANES)

    def column(c, a_ref, w_ref):
        tok = pl.ds(c * LANES, LANES)
        thr8 = [thr_ref[h, rows, tok] for h in range(PEER_HEADS)]
        e18 = [e1_ref[h, rows, tok] for h in range(PEER_HEADS)]
        for s in range(SUBLANES):
            acc = [None] * groups
            for h in range(PEER_HEADS):
                thr = jnp.broadcast_to(thr8[h][s:s + 1, :], (SUBLANES, LANES))
                e1 = jnp.broadcast_to(e18[h][s:s + 1, :], (SUBLANES, LANES))
                for g in range(groups):
                    ks = pl.ds(g * SUBLANES, SUBLANES)
                    sel = jnp.where(s2_ref[h, ks, tok] >= thr, e2_ref[h, ks, tok], 0.0)
                    acc[g] = sel * e1 if h == 0 else acc[g] + sel * e1
            gt = jnp.concatenate(acc, axis=0)
            es = slice(s * PEER_N_KEYS, (s + 1) * PEER_N_KEYS)
            av = a_ref[tok, es]
            w_ref[tok, es] = ((av + av * lax.erf(av)) * gt.T).astype(BF16)

    def stages(a_write, a_read, w_write, w_read):
        a_write[...] = _dot(h2_ref[...], down_ref[...])
        for c in range(tb // LANES):
            column(c, a_read, w_write)
        out_ref[...] += _dot(w_read[...], up_ref[...])

    @pl.when(n % 2 == 0)
    def _():
        stages(a0_ref, a1_ref, w1_ref, w0_ref)

    @pl.when(n % 2 == 1)
    def _():
        stages(a1_ref, a0_ref, w0_ref, w1_ref)

    @pl.when(j_c == nj - 1)
    def _():
        x2 = x1_ref[...] + gf_ref[...] * out_ref[...]
        out_ref[...] = _rms(x2) * fg_ref[...]


def _peer(h2, down_t, up_bf, s2t, e2t, thrt, e1t, x1, gate_f, final_g, tb):
    t, d = x1.shape
    se = SUBLANES * PEER_N_KEYS
    nj = PEER_N_EXPERTS // se
    steps = (t // tb) * nj
    stage_a = lambda n: jnp.minimum(n, steps - 1)
    stage_b = lambda n: jnp.clip(n - 1, 0, steps - 1)
    stage_c = lambda n: jnp.clip(n - 2, 0, steps - 1)
    row = lambda n: (0, 0)
    route = lambda: pl.BlockSpec((PEER_HEADS, PEER_N_KEYS, tb), lambda n: (0, 0, stage_b(n) // nj))
    return pl.pallas_call(
        functools.partial(_peer_kernel, tb=tb, nj=nj, steps=steps),
        grid=(steps + 2,),
        in_specs=[pl.BlockSpec((tb, d), lambda n: (stage_a(n) // nj, 0)),
                  pl.BlockSpec((d, se), lambda n: (0, stage_a(n) % nj)),
                  pl.BlockSpec((se, d), lambda n: (stage_c(n) % nj, 0)),
                  route(), route(), route(), route(),
                  pl.BlockSpec((tb, d), lambda n: (stage_c(n) // nj, 0),
                               pipeline_mode=pl.Buffered(1)),
                  pl.BlockSpec((1, d), row), pl.BlockSpec((1, d), row)],
        out_specs=pl.BlockSpec((tb, d), lambda n: (stage_c(n) // nj, 0)),
        out_shape=jax.ShapeDtypeStruct((t, d), F32),
        scratch_shapes=[pltpu.VMEM((tb, se), F32), pltpu.VMEM((tb, se), F32),
                        pltpu.VMEM((tb, se), BF16), pltpu.VMEM((tb, se), BF16)],
        compiler_params=pltpu.CompilerParams(dimension_semantics=("arbitrary",),
                                             vmem_limit_bytes=PEER_VMEM_LIMIT),
        name="peer",
    )(h2, down_t, up_bf, s2t, e2t, thrt, e1t, x1, gate_f, final_g)


def _pick(n, *cands):
    for cnd in cands:
        if n % cnd == 0:
            return cnd
    raise ValueError(f"unsupported token count {n}")


def _tiles(t):
    return {
        "inproj": dict(tm=_pick(t, 1024, 512, 256, 128), tn=1536),
        "gmlp": dict(tg=_pick(t, 512, 256, 128)),
        "mlstm": dict(tl=_pick(t, 512, 256, 128)),
        "outproj": dict(tm=_pick(t, 256, 128)),
        "route": dict(tc=_pick(t, 512, 256, 128)),
        "peer": dict(tb=_pick(t, 512, 256, 128)),
    }


def kernel(x, c, ada_w, ada_b, norm_mix_g, w_in, b_gates, conv_w, conv_b, gmlp_norm_g, gmlp_w_spatial, gmlp_b_spatial, mlstm_norm_g, w_out, norm_ffn_g, peer_w_query, peer_sub_keys_1, peer_sub_keys_2, peer_expert_down, peer_expert_up, final_norm_g):
    bsz, seq, d = x.shape
    assert bsz == 1 and d == D_MODEL and seq % CHUNK == 0
    assert ada_w.shape[0] == 1
    t = bsz * seq
    xt = x.reshape(t, d)
    c8 = jnp.broadcast_to(c, (SUBLANES, d))
    pad = LANES - GATE_COLS
    tiles = _tiles(t)

    for l in range(1):
        mod = _ada(c8, ada_w[l], ada_b[l][None, :])[0:1]
        shift_m, scale_m, gate_m, shift_f, scale_f, gate_f = jnp.split(mod, N_MOD, axis=-1)

        w_bf = w_in[l].astype(BF16)
        w_gate = jnp.pad(w_bf[:, PROJ_COLS:], ((0, 0), (0, pad)))
        b_gate = jnp.pad(b_gates[l], (0, pad))[None, :]
        proj, gates = _inproj(xt, norm_mix_g[l][None, :], scale_m, shift_m, w_bf, w_gate, b_gate,
                              **tiles["inproj"])

        yg = _gmlp(proj, gmlp_w_spatial[l], gmlp_b_spatial[l].T, gmlp_norm_g[l], **tiles["gmlp"])
        ym = _mlstm(proj, gates, conv_w[l], conv_b[l][None, :], mlstm_norm_g[l].reshape(1, D_MLSTM),
                    **tiles["mlstm"])

        x1, h2, s1t, s2t = _outproj(yg, ym, xt, w_out[l].astype(BF16), gate_m,
                                    norm_ffn_g[l][None, :], scale_f, shift_f,
                                    peer_w_query[l].astype(BF16), peer_sub_keys_1[l],
                                    peer_sub_keys_2[l], **tiles["outproj"])
        thrt, e1t, e2t = _route(s1t, s2t, **tiles["route"])
        fg = final_norm_g[None, :]
        down_t = (peer_expert_down[l] * math.sqrt(0.5)).astype(BF16).T
        xt = _peer(h2, down_t, peer_expert_up[l].astype(BF16),
                   s2t, e2t, thrt, e1t, x1, gate_f, fg, **tiles["peer"])
    return xt.reshape(bsz, seq, d)
```

```python
import functools
import math

import jax
import jax.numpy as jnp
from jax import lax
from jax.experimental import pallas as pl
from jax.experimental.pallas import tpu as pltpu

D_MODEL = 2048
D_GMLP = 1024
GMLP_GROUPS = 8
GMLP_GROUP_DIM = 128
CHUNK = 128
D_MLSTM = 1024
MLSTM_HEADS = 4
MLSTM_HEAD_DIM = 256
CONV_WIDTH = 4
PEER_HEADS = 8
PEER_N_KEYS = 128
PEER_N_EXPERTS = PEER_N_KEYS * PEER_N_KEYS
PEER_HALF_DIM = 128
PEER_TOPK = 16
N_MOD = 6
EPS = 1e-6
PROJ_COLS = 2 * D_GMLP + 4 * D_MLSTM
GATE_COLS = 2 * MLSTM_HEADS
LANES = 128
SUBLANES = 8
VMEM_LIMIT = 56 * 1024 * 1024
PEER_VMEM_LIMIT = 62 * 1024 * 1024

F32 = jnp.float32
BF16 = jnp.bfloat16
NEG_INF = float("-inf")
POS_INF = float("inf")
HIGHEST = lax.Precision.HIGHEST


def _params(*sem):
    return pltpu.CompilerParams(dimension_semantics=sem, vmem_limit_bytes=VMEM_LIMIT)


def _gelu(x):
    return 0.5 * x * (1.0 + lax.erf(x * (1.0 / math.sqrt(2.0))))


def _sigmoid(x):
    return 1.0 / (1.0 + jnp.exp(-x))


def _rms(x):
    return x * lax.rsqrt(jnp.mean(x * x, axis=-1, keepdims=True) + EPS)


def _dot(a, b, **kw):
    return jnp.dot(a, b, preferred_element_type=F32, **kw)


def _dot_nt(a, b, **kw):
    return lax.dot_general(a, b, (((1,), (1,)), ((), ())), preferred_element_type=F32, **kw)


def _dot_tn(a, b, **kw):
    return lax.dot_general(a, b, (((0,), (0,)), ((), ())), preferred_element_type=F32, **kw)


def _ada_kernel(c_ref, w_ref, b_ref, o_ref):
    c = c_ref[...]
    sc = c * _sigmoid(c)
    o_ref[...] = _dot(sc, w_ref[...], precision=HIGHEST) + b_ref[...]


def _ada(c8, ada_w, ada_b):
    d, n = ada_w.shape
    tn = 1024
    return pl.pallas_call(
        _ada_kernel,
        grid=(n // tn,),
        in_specs=[pl.BlockSpec((SUBLANES, d), lambda j: (0, 0)),
                  pl.BlockSpec((d, tn), lambda j: (0, j)),
                  pl.BlockSpec((1, tn), lambda j: (0, j))],
        out_specs=pl.BlockSpec((SUBLANES, tn), lambda j: (0, j)),
        out_shape=jax.ShapeDtypeStruct((SUBLANES, n), F32),
        compiler_params=_params("arbitrary"),
        name="ada",
    )(c8, ada_w, ada_b)


def _inproj_kernel(x_ref, g_ref, sc_ref, sh_ref, w_ref, wg_ref, bg_ref, proj_ref, gate_ref, h_ref):
    @pl.when(pl.program_id(1) == 0)
    def _():
        h = _rms(x_ref[...]) * g_ref[...] * (1.0 + sc_ref[...]) + sh_ref[...]
        h_ref[...] = h.astype(BF16)
        gate_ref[...] = _dot(h_ref[...], wg_ref[...]) + bg_ref[...]

    proj_ref[...] = _dot(h_ref[...], w_ref[...])


def _inproj(x, g, scale, shift, w_bf, w_gate, b_gate, tm, tn):
    t, d = x.shape
    n = PROJ_COLS
    row = lambda i, j: (0, 0)
    return pl.pallas_call(
        _inproj_kernel,
        grid=(t // tm, n // tn),
        in_specs=[pl.BlockSpec((tm, d), lambda i, j: (i, 0)),
                  pl.BlockSpec((1, d), row), pl.BlockSpec((1, d), row), pl.BlockSpec((1, d), row),
                  pl.BlockSpec((d, tn), lambda i, j: (0, j)),
                  pl.BlockSpec((d, LANES), row), pl.BlockSpec((1, LANES), row)],
        out_specs=[pl.BlockSpec((tm, tn), lambda i, j: (i, j)),
                   pl.BlockSpec((tm, LANES), lambda i, j: (i, 0))],
        out_shape=[jax.ShapeDtypeStruct((t, n), F32), jax.ShapeDtypeStruct((t, LANES), F32)],
        scratch_shapes=[pltpu.VMEM((tm, d), BF16)],
        compiler_params=_params("parallel", "arbitrary"),
        name="inproj",
    )(x, g, scale, shift, w_bf, w_gate, b_gate)


def _gmlp_kernel(u_ref, v_ref, w_ref, bt_ref, ng_ref, o_ref, *, chunks):
    row = lax.broadcasted_iota(jnp.int32, (CHUNK, CHUNK), 0)
    col = lax.broadcasted_iota(jnp.int32, (CHUNK, CHUNK), 1)
    causal = row >= col
    for g in range(GMLP_GROUPS):
        w = jnp.where(causal, w_ref[g], 0.0).astype(BF16)
        bias = bt_ref[:, g:g + 1]
        ng = ng_ref[g:g + 1, :]
        cs = slice(g * GMLP_GROUP_DIM, (g + 1) * GMLP_GROUP_DIM)
        for n in range(chunks):
            rs = slice(n * CHUNK, (n + 1) * CHUNK)
            vn = _rms(_gelu(v_ref[rs, cs])) * ng
            mixed = _dot(w, vn.astype(BF16)) + bias
            o_ref[rs, cs] = (_gelu(u_ref[rs, cs]) * mixed).astype(BF16)


def _gmlp(proj, w_spatial, b_spatial_t, norm_g, tg):
    t = proj.shape[0]
    full = lambda *s: pl.BlockSpec(s, lambda i: (0,) * len(s))
    return pl.pallas_call(
        functools.partial(_gmlp_kernel, chunks=tg // CHUNK),
        grid=(t // tg,),
        in_specs=[pl.BlockSpec((tg, D_GMLP), lambda i: (i, 0)),
                  pl.BlockSpec((tg, D_GMLP), lambda i: (i, 1)),
                  full(GMLP_GROUPS, CHUNK, CHUNK), full(CHUNK, GMLP_GROUPS),
                  full(GMLP_GROUPS, GMLP_GROUP_DIM)],
        out_specs=pl.BlockSpec((tg, D_GMLP), lambda i: (i, 0)),
        out_shape=jax.ShapeDtypeStruct((t, D_GMLP), BF16),
        compiler_params=_params("parallel"),
        name="gmlp",
    )(proj, proj, w_spatial, b_spatial_t, norm_g)


def _mlstm_kernel(qk_ref, v_ref, o_ref, gate_ref, cw_ref, cb_ref, ng_ref, out_ref,
                  xext_ref, c_ref, n_ref, m_ref, *, chunks):
    @pl.when(pl.program_id(0) == 0)
    def _():
        xext_ref[0:SUBLANES, :] = jnp.zeros((SUBLANES, 2 * D_MLSTM), F32)
        c_ref[...] = jnp.zeros_like(c_ref)
        n_ref[...] = jnp.zeros_like(n_ref)
        m_ref[...] = jnp.zeros_like(m_ref)

    def chunk(r, carry):
        rs = pl.ds(pl.multiple_of(r * CHUNK, CHUNK), CHUNK)
        _mlstm_chunk(qk_ref.at[rs], v_ref.at[rs], o_ref.at[rs], gate_ref.at[rs], cw_ref, cb_ref,
                     ng_ref, out_ref.at[rs], xext_ref, c_ref, n_ref, m_ref)
        return carry

    lax.fori_loop(0, chunks, chunk, 0)


def _mlstm_chunk(qk_ref, v_ref, o_ref, gate_ref, cw_ref, cb_ref, ng_ref, out_ref,
                 xext_ref, c_ref, n_ref, m_ref):
    L = CHUNK
    dh = MLSTM_HEAD_DIM

    xext_ref[SUBLANES:SUBLANES + L, :] = qk_ref[...]
    conv = cb_ref[...]
    for k in range(CONV_WIDTH):
        off = SUBLANES - (CONV_WIDTH - 1) + k
        conv = conv + cw_ref[k:k + 1, :] * xext_ref[off:off + L, :]
    xext_ref[0:SUBLANES, :] = xext_ref[L:L + SUBLANES, :]
    qk = conv * _sigmoid(conv)

    gates = gate_ref[...]
    log_f = jnp.minimum(gates, 0.0) - jnp.log(1.0 + jnp.exp(-jnp.abs(gates)))
    row = lax.broadcasted_iota(jnp.int32, (L, L), 0)
    col = lax.broadcasted_iota(jnp.int32, (L, L), 1)
    causal = row >= col
    tri = jnp.where(causal, 1.0, 0.0).astype(F32)
    cum = _dot(tri, log_f, precision=HIGHEST)
    lane = lax.broadcasted_iota(jnp.int32, (L, LANES), 1)
    mat = jnp.where(lane < MLSTM_HEADS, gates, cum)
    mat_t = mat.T

    for h in range(MLSTM_HEADS):
        hs = slice(h * dh, (h + 1) * dh)
        q = (qk[:, h * dh:(h + 1) * dh] * (dh ** -0.5)).astype(BF16)
        k = qk[:, D_MLSTM + h * dh:D_MLSTM + (h + 1) * dh]
        v = v_ref[:, hs].astype(BF16)
        i_col = mat[:, h:h + 1]
        b_col = mat[:, MLSTM_HEADS + h:MLSTM_HEADS + h + 1]
        i_row = mat_t[h:h + 1, :]
        b_row = mat_t[MLSTM_HEADS + h:MLSTM_HEADS + h + 1, :]
        m_prev = m_ref[h][0:1, 0:1]
        c_prev = c_ref[h]
        n_prev = n_ref[h][0:1, :]

        log_d = jnp.where(causal, b_col - b_row + i_row, NEG_INF)
        a = b_col + m_prev
        m_comb = jnp.maximum(a, jnp.max(log_d, axis=-1, keepdims=True))
        w_intra = jnp.exp(log_d - m_comb)
        w_inter = jnp.exp(a - m_comb)
        s = _dot_nt(q, k.astype(BF16)) * w_intra
        num = _dot(s.astype(BF16), v) + w_inter * _dot(q, c_prev.astype(BF16))
        qf = q.astype(F32)
        den = (jnp.sum(s, axis=-1, keepdims=True)
               + w_inter * jnp.sum(qf * n_prev, axis=-1, keepdims=True))
        hout = num / jnp.maximum(jnp.abs(den), jnp.exp(-m_comb))

        b_last = b_col[L - 1:L, :]
        log_w = b_last - b_col + i_col
        m_new = jnp.maximum(b_last + m_prev, jnp.max(log_w, axis=0, keepdims=True))
        w_state = jnp.exp(log_w - m_new)
        decay = jnp.exp(b_last + m_prev - m_new)
        kw = w_state * k
        c_ref[h] = decay * c_prev + _dot_tn(kw.astype(BF16), v)
        n_ref[h] = jnp.broadcast_to(decay * n_prev + jnp.sum(kw, axis=0, keepdims=True),
                                    (SUBLANES, dh))
        m_ref[h] = jnp.broadcast_to(m_new, (SUBLANES, LANES))

        y = _rms(hout) * ng_ref[:, hs] * _sigmoid(o_ref[:, hs])
        out_ref[:, hs] = y.astype(BF16)


def _mlstm(proj, gates, conv_w, conv_b, norm_g, tl):
    t = proj.shape[0]
    L = CHUNK
    full = lambda *s: pl.BlockSpec(s, lambda i: (0,) * len(s))
    return pl.pallas_call(
        functools.partial(_mlstm_kernel, chunks=tl // L),
        grid=(t // tl,),
        in_specs=[pl.BlockSpec((tl, 2 * D_MLSTM), lambda i: (i, 1)),
                  pl.BlockSpec((tl, D_MLSTM), lambda i: (i, 4)),
                  pl.BlockSpec((tl, D_MLSTM), lambda i: (i, 5)),
                  pl.BlockSpec((tl, LANES), lambda i: (i, 0)),
                  full(CONV_WIDTH, 2 * D_MLSTM), full(1, 2 * D_MLSTM), full(1, D_MLSTM)],
        out_specs=pl.BlockSpec((tl, D_MLSTM), lambda i: (i, 0)),
        out_shape=jax.ShapeDtypeStruct((t, D_MLSTM), BF16),
        scratch_shapes=[pltpu.VMEM((L + SUBLANES, 2 * D_MLSTM), F32),
                        pltpu.VMEM((MLSTM_HEADS, MLSTM_HEAD_DIM, MLSTM_HEAD_DIM), F32),
                        pltpu.VMEM((MLSTM_HEADS, SUBLANES, MLSTM_HEAD_DIM), F32),
                        pltpu.VMEM((MLSTM_HEADS, SUBLANES, LANES), F32)],
        compiler_params=_params("arbitrary"),
        name="mlstm",
    )(proj, proj, proj, gates, conv_w, conv_b, norm_g)


def _outproj_kernel(yg_ref, ym_ref, x_ref, wo_ref, gm_ref, g_ref, sc_ref, sh_ref, wq_ref,
                    k1_ref, k2_ref, x1_ref, h2_ref, s1_ref, s2_ref):
    mix = _dot(yg_ref[...], wo_ref[0:D_GMLP, :]) + _dot(ym_ref[...], wo_ref[D_GMLP:, :])
    x1 = x_ref[...] + gm_ref[...] * mix
    x1_ref[...] = x1
    h2f = _rms(x1) * g_ref[...] * (1.0 + sc_ref[...]) + sh_ref[...]
    h2 = h2f.astype(BF16)
    h2_ref[...] = (h2f * math.sqrt(0.5)).astype(BF16)
    q = _dot(h2, wq_ref[...]).astype(BF16)
    k1 = k1_ref[...].astype(BF16)
    k2 = k2_ref[...].astype(BF16)
    for h in range(PEER_HEADS):
        base = h * 2 * PEER_HALF_DIM
        s1_ref[h] = _dot_nt(k1, q[:, base:base + PEER_HALF_DIM])
        s2_ref[h] = _dot_nt(k2, q[:, base + PEER_HALF_DIM:base + 2 * PEER_HALF_DIM])


def _outproj(yg, ym, x, w_out, gate_m, g, scale, shift, w_query, keys1, keys2, tm):
    t, d = x.shape
    row = lambda i: (0, 0)
    tok = lambda w: pl.BlockSpec((tm, w), lambda i: (i, 0))
    score = pl.BlockSpec((PEER_HEADS, PEER_N_KEYS, tm), lambda i: (0, 0, i))
    return pl.pallas_call(
        _outproj_kernel,
        grid=(t // tm,),
        in_specs=[tok(D_GMLP), tok(D_MLSTM), tok(d),
                  pl.BlockSpec((d, d), row, pipeline_mode=pl.Buffered(1)),
                  pl.BlockSpec((1, d), row), pl.BlockSpec((1, d), row),
                  pl.BlockSpec((1, d), row), pl.BlockSpec((1, d), row),
                  pl.BlockSpec((d, d), row, pipeline_mode=pl.Buffered(1)),
                  pl.BlockSpec((PEER_N_KEYS, PEER_HALF_DIM), row),
                  pl.BlockSpec((PEER_N_KEYS, PEER_HALF_DIM), row)],
        out_specs=[tok(d), tok(d), score, score],
        out_shape=[jax.ShapeDtypeStruct((t, d), F32), jax.ShapeDtypeStruct((t, d), BF16),
                   jax.ShapeDtypeStruct((PEER_HEADS, PEER_N_KEYS, t), F32),
                   jax.ShapeDtypeStruct((PEER_HEADS, PEER_N_KEYS, t), F32)],
        compiler_params=_params("parallel"),
        name="outproj",
    )(yg, ym, x, w_out, gate_m, g, scale, shift, w_query, keys1, keys2)


def _pair_candidates():
    return [(a, b) for a in range(PEER_TOPK) for b in range(PEER_TOPK)
            if (a + 1) * (b + 1) <= PEER_TOPK]


def _sort_network(n):
    def merge(lo, hi, r):
        step = r * 2
        if step < hi - lo:
            yield from merge(lo, hi, step)
            yield from merge(lo + r, hi, step)
            yield from [(i, i + r) for i in range(lo + r, hi - r, step)]
        else:
            yield (lo, lo + r)

    def sort(lo, hi):
        if hi - lo >= 1:
            mid = lo + (hi - lo) // 2
            yield from sort(lo, mid)
            yield from sort(mid + 1, hi)
            yield from merge(lo, hi, 1)

    return list(sort(0, n - 1))


def _compare_exchange(v, i, j):
    v[i], v[j] = jnp.maximum(v[i], v[j]), jnp.minimum(v[i], v[j])


def _bitonic_sort(v):
    d = PEER_TOPK // 2
    while d >= 1:
        for i in range(PEER_TOPK):
            if i & d == 0:
                _compare_exchange(v, i, i + d)
        d //= 2


def _top16(tiles):
    v = list(tiles)
    for i, j in _sort_network(PEER_TOPK):
        _compare_exchange(v, i, j)
    for shift in (4, 2, 1):
        other = [pltpu.roll(x, shift, axis=0) for x in v]
        v = [jnp.maximum(v[i], other[PEER_TOPK - 1 - i]) for i in range(PEER_TOPK)]
        _bitonic_sort(v)
    return v


def _route_kernel(s1_ref, s2_ref, thr_ref, e1_ref, e2_ref, *, cols):
    groups = PEER_N_KEYS // SUBLANES
    sub = lax.broadcasted_iota(jnp.int32, (SUBLANES, LANES), 0)
    pairs = _pair_candidates()

    def rows(g):
        return pl.ds(g * SUBLANES, SUBLANES)

    def column(c, carry):
        tok = pl.ds(pl.multiple_of(c * LANES, LANES), LANES)
        v1s = [jnp.zeros((SUBLANES, LANES), F32)] * PEER_TOPK
        v2s = [jnp.zeros((SUBLANES, LANES), F32)] * PEER_TOPK
        for h in range(PEER_HEADS):
            t1 = _top16([s1_ref[h, rows(g), tok] for g in range(groups)])
            t2 = _top16([s2_ref[h, rows(g), tok] for g in range(groups)])
            v1s = [jnp.where(sub == h, t1[a], v1s[a]) for a in range(PEER_TOPK)]
            v2s = [jnp.where(sub == h, t2[a], v2s[a]) for a in range(PEER_TOPK)]

        cands = [v1s[a] + v2s[b] for a, b in pairs]
        row = {ab: cnd for ab, cnd in zip(pairs, cands)}
        top = [row[(0, b)] for b in range(PEER_TOPK)]
        for a in range(1, PEER_TOPK):
            for b in range(PEER_TOPK // (a + 1)):
                i = PEER_TOPK - 1 - b
                top[i] = jnp.maximum(top[i], row[(a, b)])
            _bitonic_sort(top)
        tau = top[PEER_TOPK - 1]

        e2s = [jnp.exp(v2s[b] - v2s[0]) for b in range(PEER_TOPK)]
        thr_a = [jnp.full((SUBLANES, LANES), POS_INF, F32)] * PEER_TOPK
        zs_a = [jnp.zeros((SUBLANES, LANES), F32)] * PEER_TOPK
        for (a, b), cnd in zip(pairs, cands):
            hit = cnd >= tau
            thr_a[a] = jnp.where(hit, v2s[b], thr_a[a])
            zs_a[a] = zs_a[a] + jnp.where(hit, e2s[b], 0.0)
        z = jnp.zeros((SUBLANES, LANES), F32)
        for a in range(PEER_TOPK):
            z = z + jnp.exp(v1s[a] - v1s[0]) * zs_a[a]
        inv_z = math.sqrt(0.5) / z

        for h in range(PEER_HEADS):
            def rep(x):
                return jnp.broadcast_to(x[h:h + 1, :], (SUBLANES, LANES))
            v1r = [rep(v1s[a]) for a in range(PEER_TOPK)]
            thr_r = [rep(thr_a[a]) for a in range(PEER_TOPK)]
            max2 = rep(v2s[0])
            inv_zr = rep(inv_z)
            for g in range(groups):
                s1 = s1_ref[h, rows(g), tok]
                thr = jnp.full((SUBLANES, LANES), POS_INF, F32)
                for a in range(PEER_TOPK):
                    thr = jnp.where(s1 == v1r[a], thr_r[a], thr)
                thr_ref[h, rows(g), tok] = thr
                e1_ref[h, rows(g), tok] = jnp.exp(s1 - v1r[0]) * inv_zr
                e2_ref[h, rows(g), tok] = jnp.exp(s2_ref[h, rows(g), tok] - max2)
        return carry

    lax.fori_loop(0, cols, column, 0)


def _route(s1t, s2t, tc):
    t = s1t.shape[2]
    spec = pl.BlockSpec((PEER_HEADS, PEER_N_KEYS, tc), lambda i: (0, 0, i))
    shape = jax.ShapeDtypeStruct(s1t.shape, F32)
    return pl.pallas_call(
        functools.partial(_route_kernel, cols=tc // LANES),
        grid=(t // tc,),
        in_specs=[spec, spec],
        out_specs=[spec, spec, spec],
        out_shape=[shape, shape, shape],
        compiler_params=_params("parallel"),
        name="route",
    )(s1t, s2t)


def _peer_kernel(h2_ref, down_ref, up_ref, s2_ref, e2_ref, thr_ref, e1_ref, x1_ref, gf_ref, fg_ref,
                 out_ref, a0_ref, a1_ref, w0_ref, w1_ref, *, tb, nj, steps):
    n = pl.program_id(0)
    j_b = jnp.clip(n - 1, 0, steps - 1) % nj
    j_c = jnp.clip(n - 2, 0, steps - 1) % nj
    groups = PEER_N_KEYS // SUBLANES

    @pl.when(n == 0)
    def _():
        a1_ref[...] = jnp.zeros_like(a1_ref)
        w0_ref[...] = jnp.zeros_like(w0_ref)
        w1_ref[...] = jnp.zeros_like(w1_ref)

    @pl.when(j_c == 0)
    def _():
        out_ref[...] = jnp.zeros_like(out_ref)

    rows = pl.ds(pl.multiple_of(j_b * SUBLANES, SUBLANES), SUBLANES)

    def column(c, a_ref, w_ref):
        tok = pl.ds(c * LANES, LANES)
        thr8 = [thr_ref[h, rows, tok] for h in range(PEER_HEADS)]
        e18 = [e1_ref[h, rows, tok] for h in range(PEER_HEADS)]
        for s in range(SUBLANES):
            acc = [None] * groups
            for h in range(PEER_HEADS):
                thr = jnp.broadcast_to(thr8[h][s:s + 1, :], (SUBLANES, LANES))
                e1 = jnp.broadcast_to(e18[h][s:s + 1, :], (SUBLANES, LANES))
                for g in range(groups):
                    ks = pl.ds(g * SUBLANES, SUBLANES)
                    sel = jnp.where(s2_ref[h, ks, tok] >= thr, e2_ref[h, ks, tok], 0.0)
                    acc[g] = sel * e1 if h == 0 else acc[g] + sel * e1
            gt = jnp.concatenate(acc, axis=0)
            es = slice(s * PEER_N_KEYS, (s + 1) * PEER_N_KEYS)
            av = a_ref[tok, es]
            w_ref[tok, es] = ((av + av * lax.erf(av)) * gt.T).astype(BF16)

    def stages(a_write, a_read, w_write, w_read):
        a_write[...] = _dot(h2_ref[...], down_ref[...])
        for c in range(tb // LANES):
            column(c, a_read, w_write)
        out_ref[...] += _dot(w_read[...], up_ref[...])

    @pl.when(n % 2 == 0)
    def _():
        stages(a0_ref, a1_ref, w1_ref, w0_ref)

    @pl.when(n % 2 == 1)
    def _():
        stages(a1_ref, a0_ref, w0_ref, w1_ref)

    @pl.when(j_c == nj - 1)
    def _():
        x2 = x1_ref[...] + gf_ref[...] * out_ref[...]
        out_ref[...] = _rms(x2) * fg_ref[...]


def _peer(h2, down_t, up_bf, s2t, e2t, thrt, e1t, x1, gate_f, final_g, tb):
    t, d = x1.shape
    se = SUBLANES * PEER_N_KEYS
    nj = PEER_N_EXPERTS // se
    steps = (t // tb) * nj
    stage_a = lambda n: jnp.minimum(n, steps - 1)
    stage_b = lambda n: jnp.clip(n - 1, 0, steps - 1)
    stage_c = lambda n: jnp.clip(n - 2, 0, steps - 1)
    row = lambda n: (0, 0)
    route = lambda: pl.BlockSpec((PEER_HEADS, PEER_N_KEYS, tb), lambda n: (0, 0, stage_b(n) // nj))
    return pl.pallas_call(
        functools.partial(_peer_kernel, tb=tb, nj=nj, steps=steps),
        grid=(steps + 2,),
        in_specs=[pl.BlockSpec((tb, d), lambda n: (stage_a(n) // nj, 0)),
                  pl.BlockSpec((d, se), lambda n: (0, stage_a(n) % nj)),
                  pl.BlockSpec((se, d), lambda n: (stage_c(n) % nj, 0)),
                  route(), route(), route(), route(),
                  pl.BlockSpec((tb, d), lambda n: (stage_c(n) // nj, 0),
                               pipeline_mode=pl.Buffered(1)),
                  pl.BlockSpec((1, d), row), pl.BlockSpec((1, d), row)],
        out_specs=pl.BlockSpec((tb, d), lambda n: (stage_c(n) // nj, 0)),
        out_shape=jax.ShapeDtypeStruct((t, d), F32),
        scratch_shapes=[pltpu.VMEM((tb, se), F32), pltpu.VMEM((tb, se), F32),
                        pltpu.VMEM((tb, se), BF16), pltpu.VMEM((tb, se), BF16)],
        compiler_params=pltpu.CompilerParams(dimension_semantics=("arbitrary",),
                                             vmem_limit_bytes=PEER_VMEM_LIMIT),
        name="peer",
    )(h2, down_t, up_bf, s2t, e2t, thrt, e1t, x1, gate_f, final_g)


def _pick(n, *cands):
    for cnd in cands:
        if n % cnd == 0:
            return cnd
    raise ValueError(f"unsupported token count {n}")


def _tiles(t):
    return {
        "inproj": dict(tm=_pick(t, 1024, 512, 256, 128), tn=1536),
        "gmlp": dict(tg=_pick(t, 512, 256, 128)),
        "mlstm": dict(tl=_pick(t, 512, 256, 128)),
        "outproj": dict(tm=_pick(t, 256, 128)),
        "route": dict(tc=_pick(t, 512, 256, 128)),
        "peer": dict(tb=_pick(t, 512, 256, 128)),
    }


def kernel(x, c, ada_w, ada_b, norm_mix_g, w_in, b_gates, conv_w, conv_b, gmlp_norm_g, gmlp_w_spatial, gmlp_b_spatial, mlstm_norm_g, w_out, norm_ffn_g, peer_w_query, peer_sub_keys_1, peer_sub_keys_2, peer_expert_down, peer_expert_up, final_norm_g):
    bsz, seq, d = x.shape
    assert bsz == 1 and d == D_MODEL and seq % CHUNK == 0
    assert ada_w.shape[0] == 1
    t = bsz * seq
    xt = x.reshape(t, d)
    c8 = jnp.broadcast_to(c, (SUBLANES, d))
    pad = LANES - GATE_COLS
    tiles = _tiles(t)

    for l in range(1):
        mod = _ada(c8, ada_w[l], ada_b[l][None, :])[0:1]
        shift_m, scale_m, gate_m, shift_f, scale_f, gate_f = jnp.split(mod, N_MOD, axis=-1)

        w_bf = w_in[l].astype(BF16)
        w_gate = jnp.pad(w_bf[:, PROJ_COLS:], ((0, 0), (0, pad)))
        b_gate = jnp.pad(b_gates[l], (0, pad))[None, :]
        proj, gates = _inproj(xt, norm_mix_g[l][None, :], scale_m, shift_m, w_bf, w_gate, b_gate,
                              **tiles["inproj"])

        yg = _gmlp(proj, gmlp_w_spatial[l], gmlp_b_spatial[l].T, gmlp_norm_g[l], **tiles["gmlp"])
        ym = _mlstm(proj, gates, conv_w[l], conv_b[l][None, :], mlstm_norm_g[l].reshape(1, D_MLSTM),
                    **tiles["mlstm"])

        x1, h2, s1t, s2t = _outproj(yg, ym, xt, w_out[l].astype(BF16), gate_m,
                                    norm_ffn_g[l][None, :], scale_f, shift_f,
                                    peer_w_query[l].astype(BF16), peer_sub_keys_1[l],
                                    peer_sub_keys_2[l], **tiles["outproj"])
        thrt, e1t, e2t = _route(s1t, s2t, **tiles["route"])
        fg = final_norm_g[None, :]
        xt = _peer(h2, peer_expert_down[l].astype(BF16).T, peer_expert_up[l].astype(BF16),
                   s2t, e2t, thrt, e1t, x1, gate_f, fg, **tiles["peer"])
    return xt.reshape(bsz, seq, d)
```

```python
import functools
import math

import jax
import jax.numpy as jnp
from jax import lax
from jax.experimental import pallas as pl
from jax.experimental.pallas import tpu as pltpu

D_MODEL = 2048
D_GMLP = 1024
GMLP_GROUPS = 8
GMLP_GROUP_DIM = 128
CHUNK = 128
D_MLSTM = 1024
MLSTM_HEADS = 4
MLSTM_HEAD_DIM = 256
CONV_WIDTH = 4
PEER_HEADS = 8
PEER_N_KEYS = 128
PEER_N_EXPERTS = PEER_N_KEYS * PEER_N_KEYS
PEER_HALF_DIM = 128
PEER_TOPK = 16
N_MOD = 6
EPS = 1e-6
PROJ_COLS = 2 * D_GMLP + 4 * D_MLSTM
GATE_COLS = 2 * MLSTM_HEADS
LANES = 128
SUBLANES = 8
VMEM_LIMIT = 56 * 1024 * 1024
PEER_VMEM_LIMIT = 62 * 1024 * 1024

F32 = jnp.float32
BF16 = jnp.bfloat16
NEG_INF = float("-inf")
POS_INF = float("inf")
HIGHEST = lax.Precision.HIGHEST


def _params(*sem):
    return pltpu.CompilerParams(dimension_semantics=sem, vmem_limit_bytes=VMEM_LIMIT)


def _gelu(x):
    return 0.5 * x * (1.0 + lax.erf(x * (1.0 / math.sqrt(2.0))))


def _sigmoid(x):
    return 1.0 / (1.0 + jnp.exp(-x))


def _rms(x):
    return x * lax.rsqrt(jnp.mean(x * x, axis=-1, keepdims=True) + EPS)


def _dot(a, b, **kw):
    return jnp.dot(a, b, preferred_element_type=F32, **kw)


def _dot_nt(a, b, **kw):
    return lax.dot_general(a, b, (((1,), (1,)), ((), ())), preferred_element_type=F32, **kw)


def _dot_tn(a, b, **kw):
    return lax.dot_general(a, b, (((0,), (0,)), ((), ())), preferred_element_type=F32, **kw)


def _ada_kernel(c_ref, w_ref, b_ref, o_ref):
    c = c_ref[...]
    sc = c * _sigmoid(c)
    o_ref[...] = _dot(sc, w_ref[...], precision=HIGHEST) + b_ref[...]


def _ada(c8, ada_w, ada_b):
    d, n = ada_w.shape
    tn = 1024
    return pl.pallas_call(
        _ada_kernel,
        grid=(n // tn,),
        in_specs=[pl.BlockSpec((SUBLANES, d), lambda j: (0, 0)),
                  pl.BlockSpec((d, tn), lambda j: (0, j)),
                  pl.BlockSpec((1, tn), lambda j: (0, j))],
        out_specs=pl.BlockSpec((SUBLANES, tn), lambda j: (0, j)),
        out_shape=jax.ShapeDtypeStruct((SUBLANES, n), F32),
        compiler_params=_params("arbitrary"),
        name="ada",
    )(c8, ada_w, ada_b)


def _inproj_kernel(x_ref, g_ref, sc_ref, sh_ref, w_ref, wg_ref, bg_ref, proj_ref, gate_ref, h_ref):
    @pl.when(pl.program_id(1) == 0)
    def _():
        h = _rms(x_ref[...]) * g_ref[...] * (1.0 + sc_ref[...]) + sh_ref[...]
        h_ref[...] = h.astype(BF16)
        gate_ref[...] = _dot(h_ref[...], wg_ref[...]) + bg_ref[...]

    proj_ref[...] = _dot(h_ref[...], w_ref[...])


def _inproj(x, g, scale, shift, w_bf, w_gate, b_gate, tm, tn):
    t, d = x.shape
    n = PROJ_COLS
    row = lambda i, j: (0, 0)
    return pl.pallas_call(
        _inproj_kernel,
        grid=(t // tm, n // tn),
        in_specs=[pl.BlockSpec((tm, d), lambda i, j: (i, 0)),
                  pl.BlockSpec((1, d), row), pl.BlockSpec((1, d), row), pl.BlockSpec((1, d), row),
                  pl.BlockSpec((d, tn), lambda i, j: (0, j)),
                  pl.BlockSpec((d, LANES), row), pl.BlockSpec((1, LANES), row)],
        out_specs=[pl.BlockSpec((tm, tn), lambda i, j: (i, j)),
                   pl.BlockSpec((tm, LANES), lambda i, j: (i, 0))],
        out_shape=[jax.ShapeDtypeStruct((t, n), F32), jax.ShapeDtypeStruct((t, LANES), F32)],
        scratch_shapes=[pltpu.VMEM((tm, d), BF16)],
        compiler_params=_params("parallel", "arbitrary"),
        name="inproj",
    )(x, g, scale, shift, w_bf, w_gate, b_gate)


def _gmlp_kernel(u_ref, v_ref, w_ref, bt_ref, ng_ref, o_ref, *, chunks):
    row = lax.broadcasted_iota(jnp.int32, (CHUNK, CHUNK), 0)
    col = lax.broadcasted_iota(jnp.int32, (CHUNK, CHUNK), 1)
    causal = row >= col
    for g in range(GMLP_GROUPS):
        w = jnp.where(causal, w_ref[g], 0.0).astype(BF16)
        bias = bt_ref[:, g:g + 1]
        ng = ng_ref[g:g + 1, :]
        cs = slice(g * GMLP_GROUP_DIM, (g + 1) * GMLP_GROUP_DIM)
        for n in range(chunks):
            rs = slice(n * CHUNK, (n + 1) * CHUNK)
            vn = _rms(_gelu(v_ref[rs, cs])) * ng
            mixed = _dot(w, vn.astype(BF16)) + bias
            o_ref[rs, cs] = (_gelu(u_ref[rs, cs]) * mixed).astype(BF16)


def _gmlp(proj, w_spatial, b_spatial_t, norm_g, tg):
    t = proj.shape[0]
    full = lambda *s: pl.BlockSpec(s, lambda i: (0,) * len(s))
    return pl.pallas_call(
        functools.partial(_gmlp_kernel, chunks=tg // CHUNK),
        grid=(t // tg,),
        in_specs=[pl.BlockSpec((tg, D_GMLP), lambda i: (i, 0)),
                  pl.BlockSpec((tg, D_GMLP), lambda i: (i, 1)),
                  full(GMLP_GROUPS, CHUNK, CHUNK), full(CHUNK, GMLP_GROUPS),
                  full(GMLP_GROUPS, GMLP_GROUP_DIM)],
        out_specs=pl.BlockSpec((tg, D_GMLP), lambda i: (i, 0)),
        out_shape=jax.ShapeDtypeStruct((t, D_GMLP), BF16),
        compiler_params=_params("parallel"),
        name="gmlp",
    )(proj, proj, w_spatial, b_spatial_t, norm_g)


def _mlstm_kernel(qk_ref, v_ref, o_ref, gate_ref, cw_ref, cb_ref, ng_ref, out_ref,
                  xext_ref, c_ref, n_ref, m_ref, *, chunks):
    @pl.when(pl.program_id(0) == 0)
    def _():
        xext_ref[0:SUBLANES, :] = jnp.zeros((SUBLANES, 2 * D_MLSTM), F32)
        c_ref[...] = jnp.zeros_like(c_ref)
        n_ref[...] = jnp.zeros_like(n_ref)
        m_ref[...] = jnp.zeros_like(m_ref)

    def chunk(r, carry):
        rs = pl.ds(pl.multiple_of(r * CHUNK, CHUNK), CHUNK)
        _mlstm_chunk(qk_ref.at[rs], v_ref.at[rs], o_ref.at[rs], gate_ref.at[rs], cw_ref, cb_ref,
                     ng_ref, out_ref.at[rs], xext_ref, c_ref, n_ref, m_ref)
        return carry

    lax.fori_loop(0, chunks, chunk, 0)


def _mlstm_chunk(qk_ref, v_ref, o_ref, gate_ref, cw_ref, cb_ref, ng_ref, out_ref,
                 xext_ref, c_ref, n_ref, m_ref):
    L = CHUNK
    dh = MLSTM_HEAD_DIM

    xext_ref[SUBLANES:SUBLANES + L, :] = qk_ref[...]
    conv = cb_ref[...]
    for k in range(CONV_WIDTH):
        off = SUBLANES - (CONV_WIDTH - 1) + k
        conv = conv + cw_ref[k:k + 1, :] * xext_ref[off:off + L, :]
    xext_ref[0:SUBLANES, :] = xext_ref[L:L + SUBLANES, :]
    qk = conv * _sigmoid(conv)

    gates = gate_ref[...]
    log_f = jnp.minimum(gates, 0.0) - jnp.log(1.0 + jnp.exp(-jnp.abs(gates)))
    row = lax.broadcasted_iota(jnp.int32, (L, L), 0)
    col = lax.broadcasted_iota(jnp.int32, (L, L), 1)
    causal = row >= col
    tri = jnp.where(causal, 1.0, 0.0).astype(F32)
    cum = _dot(tri, log_f, precision=HIGHEST)
    lane = lax.broadcasted_iota(jnp.int32, (L, LANES), 1)
    mat = jnp.where(lane < MLSTM_HEADS, gates, cum)
    mat_t = mat.T

    for h in range(MLSTM_HEADS):
        hs = slice(h * dh, (h + 1) * dh)
        q = (qk[:, h * dh:(h + 1) * dh] * (dh ** -0.5)).astype(BF16)
        k = qk[:, D_MLSTM + h * dh:D_MLSTM + (h + 1) * dh]
        v = v_ref[:, hs].astype(BF16)
        i_col = mat[:, h:h + 1]
        b_col = mat[:, MLSTM_HEADS + h:MLSTM_HEADS + h + 1]
        i_row = mat_t[h:h + 1, :]
        b_row = mat_t[MLSTM_HEADS + h:MLSTM_HEADS + h + 1, :]
        m_prev = m_ref[h][0:1, 0:1]
        c_prev = c_ref[h]
        n_prev = n_ref[h][0:1, :]

        log_d = jnp.where(causal, b_col - b_row + i_row, NEG_INF)
        a = b_col + m_prev
        m_comb = jnp.maximum(a, jnp.max(log_d, axis=-1, keepdims=True))
        w_intra = jnp.exp(log_d - m_comb)
        w_inter = jnp.exp(a - m_comb)
        s = _dot_nt(q, k.astype(BF16)) * w_intra
        num = _dot(s.astype(BF16), v) + w_inter * _dot(q, c_prev.astype(BF16))
        qf = q.astype(F32)
        den = (jnp.sum(s, axis=-1, keepdims=True)
               + w_inter * jnp.sum(qf * n_prev, axis=-1, keepdims=True))
        hout = num / jnp.maximum(jnp.abs(den), jnp.exp(-m_comb))

        b_last = b_col[L - 1:L, :]
        log_w = b_last - b_col + i_col
        m_new = jnp.maximum(b_last + m_prev, jnp.max(log_w, axis=0, keepdims=True))
        w_state = jnp.exp(log_w - m_new)
        decay = jnp.exp(b_last + m_prev - m_new)
        kw = w_state * k
        c_ref[h] = decay * c_prev + _dot_tn(kw.astype(BF16), v)
        n_ref[h] = jnp.broadcast_to(decay * n_prev + jnp.sum(kw, axis=0, keepdims=True),
                                    (SUBLANES, dh))
        m_ref[h] = jnp.broadcast_to(m_new, (SUBLANES, LANES))

        y = _rms(hout) * ng_ref[:, hs] * _sigmoid(o_ref[:, hs])
        out_ref[:, hs] = y.astype(BF16)


def _mlstm(proj, gates, conv_w, conv_b, norm_g, tl):
    t = proj.shape[0]
    L = CHUNK
    full = lambda *s: pl.BlockSpec(s, lambda i: (0,) * len(s))
    return pl.pallas_call(
        functools.partial(_mlstm_kernel, chunks=tl // L),
        grid=(t // tl,),
        in_specs=[pl.BlockSpec((tl, 2 * D_MLSTM), lambda i: (i, 1)),
                  pl.BlockSpec((tl, D_MLSTM), lambda i: (i, 4)),
                  pl.BlockSpec((tl, D_MLSTM), lambda i: (i, 5)),
                  pl.BlockSpec((tl, LANES), lambda i: (i, 0)),
                  full(CONV_WIDTH, 2 * D_MLSTM), full(1, 2 * D_MLSTM), full(1, D_MLSTM)],
        out_specs=pl.BlockSpec((tl, D_MLSTM), lambda i: (i, 0)),
        out_shape=jax.ShapeDtypeStruct((t, D_MLSTM), BF16),
        scratch_shapes=[pltpu.VMEM((L + SUBLANES, 2 * D_MLSTM), F32),
                        pltpu.VMEM((MLSTM_HEADS, MLSTM_HEAD_DIM, MLSTM_HEAD_DIM), F32),
                        pltpu.VMEM((MLSTM_HEADS, SUBLANES, MLSTM_HEAD_DIM), F32),
                        pltpu.VMEM((MLSTM_HEADS, SUBLANES, LANES), F32)],
        compiler_params=_params("arbitrary"),
        name="mlstm",
    )(proj, proj, proj, gates, conv_w, conv_b, norm_g)


def _outproj_kernel(yg_ref, ym_ref, x_ref, wo_ref, gm_ref, g_ref, sc_ref, sh_ref, wq_ref,
                    k1_ref, k2_ref, x1_ref, h2_ref, s1_ref, s2_ref):
    mix = _dot(yg_ref[...], wo_ref[0:D_GMLP, :]) + _dot(ym_ref[...], wo_ref[D_GMLP:, :])
    x1 = x_ref[...] + gm_ref[...] * mix
    x1_ref[...] = x1
    h2f = _rms(x1) * g_ref[...] * (1.0 + sc_ref[...]) + sh_ref[...]
    h2 = h2f.astype(BF16)
    h2_ref[...] = (h2f * math.sqrt(0.5)).astype(BF16)
    q = _dot(h2, wq_ref[...]).astype(BF16)
    k1 = k1_ref[...].astype(BF16)
    k2 = k2_ref[...].astype(BF16)
    for h in range(PEER_HEADS):
        base = h * 2 * PEER_HALF_DIM
        s1_ref[h] = _dot_nt(k1, q[:, base:base + PEER_HALF_DIM])
        s2_ref[h] = _dot_nt(k2, q[:, base + PEER_HALF_DIM:base + 2 * PEER_HALF_DIM])


def _outproj(yg, ym, x, w_out, gate_m, g, scale, shift, w_query, keys1, keys2, tm):
    t, d = x.shape
    row = lambda i: (0, 0)
    tok = lambda w: pl.BlockSpec((tm, w), lambda i: (i, 0))
    score = pl.BlockSpec((PEER_HEADS, PEER_N_KEYS, tm), lambda i: (0, 0, i))
    return pl.pallas_call(
        _outproj_kernel,
        grid=(t // tm,),
        in_specs=[tok(D_GMLP), tok(D_MLSTM), tok(d),
                  pl.BlockSpec((d, d), row, pipeline_mode=pl.Buffered(1)),
                  pl.BlockSpec((1, d), row), pl.BlockSpec((1, d), row),
                  pl.BlockSpec((1, d), row), pl.BlockSpec((1, d), row),
                  pl.BlockSpec((d, d), row, pipeline_mode=pl.Buffered(1)),
                  pl.BlockSpec((PEER_N_KEYS, PEER_HALF_DIM), row),
                  pl.BlockSpec((PEER_N_KEYS, PEER_HALF_DIM), row)],
        out_specs=[tok(d), tok(d), score, score],
        out_shape=[jax.ShapeDtypeStruct((t, d), F32), jax.ShapeDtypeStruct((t, d), BF16),
                   jax.ShapeDtypeStruct((PEER_HEADS, PEER_N_KEYS, t), F32),
                   jax.ShapeDtypeStruct((PEER_HEADS, PEER_N_KEYS, t), F32)],
        compiler_params=_params("parallel"),
        name="outproj",
    )(yg, ym, x, w_out, gate_m, g, scale, shift, w_query, keys1, keys2)


def _pair_candidates():
    return [(a, b) for a in range(PEER_TOPK) for b in range(PEER_TOPK)
            if (a + 1) * (b + 1) <= PEER_TOPK]


def _sort_network(n):
    def merge(lo, hi, r):
        step = r * 2
        if step < hi - lo:
            yield from merge(lo, hi, step)
            yield from merge(lo + r, hi, step)
            yield from [(i, i + r) for i in range(lo + r, hi - r, step)]
        else:
            yield (lo, lo + r)

    def sort(lo, hi):
        if hi - lo >= 1:
            mid = lo + (hi - lo) // 2
            yield from sort(lo, mid)
            yield from sort(mid + 1, hi)
            yield from merge(lo, hi, 1)

    return list(sort(0, n - 1))


def _compare_exchange(v, i, j):
    v[i], v[j] = jnp.maximum(v[i], v[j]), jnp.minimum(v[i], v[j])


def _bitonic_sort(v):
    d = PEER_TOPK // 2
    while d >= 1:
        for i in range(PEER_TOPK):
            if i & d == 0:
                _compare_exchange(v, i, i + d)
        d //= 2


def _top16(tiles):
    v = list(tiles)
    for i, j in _sort_network(PEER_TOPK):
        _compare_exchange(v, i, j)
    for shift in (4, 2, 1):
        other = [pltpu.roll(x, shift, axis=0) for x in v]
        v = [jnp.maximum(v[i], other[PEER_TOPK - 1 - i]) for i in range(PEER_TOPK)]
        _bitonic_sort(v)
    return v


def _route_kernel(s1_ref, s2_ref, thr_ref, e1_ref, e2_ref, *, cols):
    groups = PEER_N_KEYS // SUBLANES
    sub = lax.broadcasted_iota(jnp.int32, (SUBLANES, LANES), 0)
    pairs = _pair_candidates()

    def rows(g):
        return pl.ds(g * SUBLANES, SUBLANES)

    def column(c, carry):
        tok = pl.ds(pl.multiple_of(c * LANES, LANES), LANES)
        v1s = [jnp.zeros((SUBLANES, LANES), F32)] * PEER_TOPK
        v2s = [jnp.zeros((SUBLANES, LANES), F32)] * PEER_TOPK
        for h in range(PEER_HEADS):
            t1 = _top16([s1_ref[h, rows(g), tok] for g in range(groups)])
            t2 = _top16([s2_ref[h, rows(g), tok] for g in range(groups)])
            v1s = [jnp.where(sub == h, t1[a], v1s[a]) for a in range(PEER_TOPK)]
            v2s = [jnp.where(sub == h, t2[a], v2s[a]) for a in range(PEER_TOPK)]

        cands = [v1s[a] + v2s[b] for a, b in pairs]
        row = {ab: cnd for ab, cnd in zip(pairs, cands)}
        top = [row[(0, b)] for b in range(PEER_TOPK)]
        for a in range(1, PEER_TOPK):
            for b in range(PEER_TOPK // (a + 1)):
                i = PEER_TOPK - 1 - b
                top[i] = jnp.maximum(top[i], row[(a, b)])
            _bitonic_sort(top)
        tau = top[PEER_TOPK - 1]

        e2s = [jnp.exp(v2s[b] - v2s[0]) for b in range(PEER_TOPK)]
        thr_a = [jnp.full((SUBLANES, LANES), POS_INF, F32)] * PEER_TOPK
        zs_a = [jnp.zeros((SUBLANES, LANES), F32)] * PEER_TOPK
        for (a, b), cnd in zip(pairs, cands):
            hit = cnd >= tau
            thr_a[a] = jnp.where(hit, v2s[b], thr_a[a])
            zs_a[a] = zs_a[a] + jnp.where(hit, e2s[b], 0.0)
        z = jnp.zeros((SUBLANES, LANES), F32)
        for a in range(PEER_TOPK):
            z = z + jnp.exp(v1s[a] - v1s[0]) * zs_a[a]
        inv_z = math.sqrt(0.5) / z

        for h in range(PEER_HEADS):
            def rep(x):
                return jnp.broadcast_to(x[h:h + 1, :], (SUBLANES, LANES))
            v1r = [rep(v1s[a]) for a in range(PEER_TOPK)]
            thr_r = [rep(thr_a[a]) for a in range(PEER_TOPK)]
            max2 = rep(v2s[0])
            inv_zr = rep(inv_z)
            for g in range(groups):
                s1 = s1_ref[h, rows(g), tok]
                thr = jnp.full((SUBLANES, LANES), POS_INF, F32)
                for a in range(PEER_TOPK):
                    thr = jnp.where(s1 == v1r[a], thr_r[a], thr)
                thr_ref[h, rows(g), tok] = thr
                e1_ref[h, rows(g), tok] = jnp.exp(s1 - v1r[0]) * inv_zr
                e2_ref[h, rows(g), tok] = jnp.exp(s2_ref[h, rows(g), tok] - max2)
        return carry

    lax.fori_loop(0, cols, column, 0)


def _route(s1t, s2t, tc):
    t = s1t.shape[2]
    spec = pl.BlockSpec((PEER_HEADS, PEER_N_KEYS, tc), lambda i: (0, 0, i))
    shape = jax.ShapeDtypeStruct(s1t.shape, F32)
    return pl.pallas_call(
        functools.partial(_route_kernel, cols=tc // LANES),
        grid=(t // tc,),
        in_specs=[spec, spec],
        out_specs=[spec, spec, spec],
        out_shape=[shape, shape, shape],
        compiler_params=_params("parallel"),
        name="route",
    )(s1t, s2t)


def _peer_kernel(h2_ref, down_ref, up_ref, s2_ref, e2_ref, thr_ref, e1_ref, x1_ref, gf_ref, fg_ref,
                 out_ref, a0_ref, a1_ref, w0_ref, w1_ref, *, tb, nj, steps):
    n = pl.program_id(0)
    j_b = jnp.clip(n - 1, 0, steps - 1) % nj
    j_c = jnp.clip(n - 2, 0, steps - 1) % nj
    groups = PEER_N_KEYS // SUBLANES

    @pl.when(n == 0)
    def _():
        a1_ref[...] = jnp.zeros_like(a1_ref)
        w0_ref[...] = jnp.zeros_like(w0_ref)
        w1_ref[...] = jnp.zeros_like(w1_ref)

    @pl.when(j_c == 0)
    def _():
        out_ref[...] = jnp.zeros_like(out_ref)

    rows = pl.ds(pl.multiple_of(j_b * SUBLANES, SUBLANES), SUBLANES)

    def column(c, a_ref, w_ref):
        tok = pl.ds(c * LANES, LANES)
        thr8 = [thr_ref[h, rows, tok] for h in range(PEER_HEADS)]
        e18 = [e1_ref[h, rows, tok] for h in range(PEER_HEADS)]
        for s in range(SUBLANES):
            acc = [None] * groups
            for h in range(PEER_HEADS):
                thr = jnp.broadcast_to(thr8[h][s:s + 1, :], (SUBLANES, LANES))
                e1 = jnp.broadcast_to(e18[h][s:s + 1, :], (SUBLANES, LANES))
                for g in range(groups):
                    ks = pl.ds(g * SUBLANES, SUBLANES)
                    sel = jnp.where(s2_ref[h, ks, tok] >= thr, e2_ref[h, ks, tok], 0.0)
                    acc[g] = sel * e1 if h == 0 else acc[g] + sel * e1
            gt = jnp.concatenate(acc, axis=0)
            es = slice(s * PEER_N_KEYS, (s + 1) * PEER_N_KEYS)
            w_ref[tok, es] = (a_ref[tok, es].astype(F32) * gt.T).astype(BF16)

    def stages(a_write, a_read, w_write, w_read):
        av = _dot(h2_ref[...], down_ref[...])
        a_write[...] = (av + av * lax.erf(av)).astype(BF16)
        for c in range(tb // LANES):
            column(c, a_read, w_write)
        out_ref[...] += _dot(w_read[...], up_ref[...])

    @pl.when(n % 2 == 0)
    def _():
        stages(a0_ref, a1_ref, w1_ref, w0_ref)

    @pl.when(n % 2 == 1)
    def _():
        stages(a1_ref, a0_ref, w0_ref, w1_ref)

    @pl.when(j_c == nj - 1)
    def _():
        x2 = x1_ref[...] + gf_ref[...] * out_ref[...]
        out_ref[...] = _rms(x2) * fg_ref[...]


def _peer(h2, down_t, up_bf, s2t, e2t, thrt, e1t, x1, gate_f, final_g, tb):
    t, d = x1.shape
    se = SUBLANES * PEER_N_KEYS
    nj = PEER_N_EXPERTS // se
    steps = (t // tb) * nj
    stage_a = lambda n: jnp.minimum(n, steps - 1)
    stage_b = lambda n: jnp.clip(n - 1, 0, steps - 1)
    stage_c = lambda n: jnp.clip(n - 2, 0, steps - 1)
    row = lambda n: (0, 0)
    route = lambda: pl.BlockSpec((PEER_HEADS, PEER_N_KEYS, tb), lambda n: (0, 0, stage_b(n) // nj))
    return pl.pallas_call(
        functools.partial(_peer_kernel, tb=tb, nj=nj, steps=steps),
        grid=(steps + 2,),
        in_specs=[pl.BlockSpec((tb, d), lambda n: (stage_a(n) // nj, 0)),
                  pl.BlockSpec((d, se), lambda n: (0, stage_a(n) % nj)),
                  pl.BlockSpec((se, d), lambda n: (stage_c(n) % nj, 0)),
                  route(), route(), route(), route(),
                  pl.BlockSpec((tb, d), lambda n: (stage_c(n) // nj, 0)),
                  pl.BlockSpec((1, d), row), pl.BlockSpec((1, d), row)],
        out_specs=pl.BlockSpec((tb, d), lambda n: (stage_c(n) // nj, 0)),
        out_shape=jax.ShapeDtypeStruct((t, d), F32),
        scratch_shapes=[pltpu.VMEM((tb, se), BF16), pltpu.VMEM((tb, se), BF16),
                        pltpu.VMEM((tb, se), BF16), pltpu.VMEM((tb, se), BF16)],
        compiler_params=pltpu.CompilerParams(dimension_semantics=("arbitrary",),
                                             vmem_limit_bytes=PEER_VMEM_LIMIT),
        name="peer",
    )(h2, down_t, up_bf, s2t, e2t, thrt, e1t, x1, gate_f, final_g)


def _pick(n, *cands):
    for cnd in cands:
        if n % cnd == 0:
            return cnd
    raise ValueError(f"unsupported token count {n}")


def _tiles(t):
    return {
        "inproj": dict(tm=_pick(t, 1024, 512, 256, 128), tn=1536),
        "gmlp": dict(tg=_pick(t, 512, 256, 128)),
        "mlstm": dict(tl=_pick(t, 512, 256, 128)),
        "outproj": dict(tm=_pick(t, 256, 128)),
        "route": dict(tc=_pick(t, 512, 256, 128)),
        "peer": dict(tb=_pick(t, 512, 256, 128)),
    }


def kernel(x, c, ada_w, ada_b, norm_mix_g, w_in, b_gates, conv_w, conv_b, gmlp_norm_g, gmlp_w_spatial, gmlp_b_spatial, mlstm_norm_g, w_out, norm_ffn_g, peer_w_query, peer_sub_keys_1, peer_sub_keys_2, peer_expert_down, peer_expert_up, final_norm_g):
    bsz, seq, d = x.shape
    assert bsz == 1 and d == D_MODEL and seq % CHUNK == 0
    assert ada_w.shape[0] == 1
    t = bsz * seq
    xt = x.reshape(t, d)
    c8 = jnp.broadcast_to(c, (SUBLANES, d))
    pad = LANES - GATE_COLS
    tiles = _tiles(t)

    for l in range(1):
        mod = _ada(c8, ada_w[l], ada_b[l][None, :])[0:1]
        shift_m, scale_m, gate_m, shift_f, scale_f, gate_f = jnp.split(mod, N_MOD, axis=-1)

        w_bf = w_in[l].astype(BF16)
        w_gate = jnp.pad(w_bf[:, PROJ_COLS:], ((0, 0), (0, pad)))
        b_gate = jnp.pad(b_gates[l], (0, pad))[None, :]
        proj, gates = _inproj(xt, norm_mix_g[l][None, :], scale_m, shift_m, w_bf, w_gate, b_gate,
                              **tiles["inproj"])

        yg = _gmlp(proj, gmlp_w_spatial[l], gmlp_b_spatial[l].T, gmlp_norm_g[l], **tiles["gmlp"])
        ym = _mlstm(proj, gates, conv_w[l], conv_b[l][None, :], mlstm_norm_g[l].reshape(1, D_MLSTM),
                    **tiles["mlstm"])

        x1, h2, s1t, s2t = _outproj(yg, ym, xt, w_out[l].astype(BF16), gate_m,
                                    norm_ffn_g[l][None, :], scale_f, shift_f,
                                    peer_w_query[l].astype(BF16), peer_sub_keys_1[l],
                                    peer_sub_keys_2[l], **tiles["outproj"])
        thrt, e1t, e2t = _route(s1t, s2t, **tiles["route"])
        fg = final_norm_g[None, :]
        xt = _peer(h2, peer_expert_down[l].astype(BF16).T, peer_expert_up[l].astype(BF16),
                   s2t, e2t, thrt, e1t, x1, gate_f, fg, **tiles["peer"])
    return xt.reshape(bsz, seq, d)
```

```python
import functools
import math

import jax
import jax.numpy as jnp
from jax import lax
from jax.experimental import pallas as pl
from jax.experimental.pallas import tpu as pltpu

D_MODEL = 2048
D_GMLP = 1024
GMLP_GROUPS = 8
GMLP_GROUP_DIM = 128
CHUNK = 128
D_MLSTM = 1024
MLSTM_HEADS = 4
MLSTM_HEAD_DIM = 256
CONV_WIDTH = 4
PEER_HEADS = 8
PEER_N_KEYS = 128
PEER_N_EXPERTS = PEER_N_KEYS * PEER_N_KEYS
PEER_HALF_DIM = 128
PEER_TOPK = 16
N_MOD = 6
EPS = 1e-6
PROJ_COLS = 2 * D_GMLP + 4 * D_MLSTM
GATE_COLS = 2 * MLSTM_HEADS
LANES = 128
SUBLANES = 8
VMEM_LIMIT = 56 * 1024 * 1024
PEER_VMEM_LIMIT = 62 * 1024 * 1024

F32 = jnp.float32
BF16 = jnp.bfloat16
NEG_INF = float("-inf")
POS_INF = float("inf")
HIGHEST = lax.Precision.HIGHEST


def _params(*sem):
    return pltpu.CompilerParams(dimension_semantics=sem, vmem_limit_bytes=VMEM_LIMIT)


def _gelu(x):
    return 0.5 * x * (1.0 + lax.erf(x * (1.0 / math.sqrt(2.0))))


def _sigmoid(x):
    return 1.0 / (1.0 + jnp.exp(-x))


def _rms(x):
    return x * lax.rsqrt(jnp.mean(x * x, axis=-1, keepdims=True) + EPS)


def _dot(a, b, **kw):
    return jnp.dot(a, b, preferred_element_type=F32, **kw)


def _dot_nt(a, b, **kw):
    return lax.dot_general(a, b, (((1,), (1,)), ((), ())), preferred_element_type=F32, **kw)


def _dot_tn(a, b, **kw):
    return lax.dot_general(a, b, (((0,), (0,)), ((), ())), preferred_element_type=F32, **kw)


def _ada_kernel(c_ref, w_ref, b_ref, o_ref):
    c = c_ref[...]
    sc = c * _sigmoid(c)
    o_ref[...] = _dot(sc, w_ref[...], precision=HIGHEST) + b_ref[...]


def _ada(c8, ada_w, ada_b):
    d, n = ada_w.shape
    tn = 1024
    return pl.pallas_call(
        _ada_kernel,
        grid=(n // tn,),
        in_specs=[pl.BlockSpec((SUBLANES, d), lambda j: (0, 0)),
                  pl.BlockSpec((d, tn), lambda j: (0, j)),
                  pl.BlockSpec((1, tn), lambda j: (0, j))],
        out_specs=pl.BlockSpec((SUBLANES, tn), lambda j: (0, j)),
        out_shape=jax.ShapeDtypeStruct((SUBLANES, n), F32),
        compiler_params=_params("arbitrary"),
        name="ada",
    )(c8, ada_w, ada_b)


def _inproj_kernel(x_ref, g_ref, sc_ref, sh_ref, w_ref, wg_ref, bg_ref, proj_ref, gate_ref, h_ref):
    @pl.when(pl.program_id(1) == 0)
    def _():
        h = _rms(x_ref[...]) * g_ref[...] * (1.0 + sc_ref[...]) + sh_ref[...]
        h_ref[...] = h.astype(BF16)
        gate_ref[...] = _dot(h_ref[...], wg_ref[...]) + bg_ref[...]

    proj_ref[...] = _dot(h_ref[...], w_ref[...]).astype(BF16)


def _inproj(x, g, scale, shift, w_bf, w_gate, b_gate, tm, tn):
    t, d = x.shape
    n = PROJ_COLS
    row = lambda i, j: (0, 0)
    return pl.pallas_call(
        _inproj_kernel,
        grid=(t // tm, n // tn),
        in_specs=[pl.BlockSpec((tm, d), lambda i, j: (i, 0)),
                  pl.BlockSpec((1, d), row), pl.BlockSpec((1, d), row), pl.BlockSpec((1, d), row),
                  pl.BlockSpec((d, tn), lambda i, j: (0, j)),
                  pl.BlockSpec((d, LANES), row), pl.BlockSpec((1, LANES), row)],
        out_specs=[pl.BlockSpec((tm, tn), lambda i, j: (i, j)),
                   pl.BlockSpec((tm, LANES), lambda i, j: (i, 0))],
        out_shape=[jax.ShapeDtypeStruct((t, n), BF16), jax.ShapeDtypeStruct((t, LANES), F32)],
        scratch_shapes=[pltpu.VMEM((tm, d), BF16)],
        compiler_params=_params("parallel", "arbitrary"),
        name="inproj",
    )(x, g, scale, shift, w_bf, w_gate, b_gate)


def _gmlp_kernel(u_ref, v_ref, w_ref, bt_ref, ng_ref, o_ref, *, chunks):
    row = lax.broadcasted_iota(jnp.int32, (CHUNK, CHUNK), 0)
    col = lax.broadcasted_iota(jnp.int32, (CHUNK, CHUNK), 1)
    causal = row >= col
    for g in range(GMLP_GROUPS):
        w = jnp.where(causal, w_ref[g], 0.0).astype(BF16)
        bias = bt_ref[:, g:g + 1]
        ng = ng_ref[g:g + 1, :]
        cs = slice(g * GMLP_GROUP_DIM, (g + 1) * GMLP_GROUP_DIM)
        for n in range(chunks):
            rs = slice(n * CHUNK, (n + 1) * CHUNK)
            vn = _rms(_gelu(v_ref[rs, cs].astype(F32))) * ng
            mixed = _dot(w, vn.astype(BF16)) + bias
            o_ref[rs, cs] = (_gelu(u_ref[rs, cs].astype(F32)) * mixed).astype(BF16)


def _gmlp(proj, w_spatial, b_spatial_t, norm_g, tg):
    t = proj.shape[0]
    full = lambda *s: pl.BlockSpec(s, lambda i: (0,) * len(s))
    return pl.pallas_call(
        functools.partial(_gmlp_kernel, chunks=tg // CHUNK),
        grid=(t // tg,),
        in_specs=[pl.BlockSpec((tg, D_GMLP), lambda i: (i, 0)),
                  pl.BlockSpec((tg, D_GMLP), lambda i: (i, 1)),
                  full(GMLP_GROUPS, CHUNK, CHUNK), full(CHUNK, GMLP_GROUPS),
                  full(GMLP_GROUPS, GMLP_GROUP_DIM)],
        out_specs=pl.BlockSpec((tg, D_GMLP), lambda i: (i, 0)),
        out_shape=jax.ShapeDtypeStruct((t, D_GMLP), BF16),
        compiler_params=_params("parallel"),
        name="gmlp",
    )(proj, proj, w_spatial, b_spatial_t, norm_g)


def _mlstm_kernel(qk_ref, v_ref, o_ref, gate_ref, cw_ref, cb_ref, ng_ref, out_ref,
                  xext_ref, c_ref, n_ref, m_ref, *, chunks):
    @pl.when(pl.program_id(0) == 0)
    def _():
        xext_ref[0:SUBLANES, :] = jnp.zeros((SUBLANES, 2 * D_MLSTM), F32)
        c_ref[...] = jnp.zeros_like(c_ref)
        n_ref[...] = jnp.zeros_like(n_ref)
        m_ref[...] = jnp.zeros_like(m_ref)

    def chunk(r, carry):
        rs = pl.ds(pl.multiple_of(r * CHUNK, CHUNK), CHUNK)
        _mlstm_chunk(qk_ref.at[rs], v_ref.at[rs], o_ref.at[rs], gate_ref.at[rs], cw_ref, cb_ref,
                     ng_ref, out_ref.at[rs], xext_ref, c_ref, n_ref, m_ref)
        return carry

    lax.fori_loop(0, chunks, chunk, 0)


def _mlstm_chunk(qk_ref, v_ref, o_ref, gate_ref, cw_ref, cb_ref, ng_ref, out_ref,
                 xext_ref, c_ref, n_ref, m_ref):
    L = CHUNK
    dh = MLSTM_HEAD_DIM

    xext_ref[SUBLANES:SUBLANES + L, :] = qk_ref[...].astype(F32)
    conv = cb_ref[...]
    for k in range(CONV_WIDTH):
        off = SUBLANES - (CONV_WIDTH - 1) + k
        conv = conv + cw_ref[k:k + 1, :] * xext_ref[off:off + L, :]
    xext_ref[0:SUBLANES, :] = xext_ref[L:L + SUBLANES, :]
    qk = conv * _sigmoid(conv)

    gates = gate_ref[...]
    log_f = jnp.minimum(gates, 0.0) - jnp.log(1.0 + jnp.exp(-jnp.abs(gates)))
    row = lax.broadcasted_iota(jnp.int32, (L, L), 0)
    col = lax.broadcasted_iota(jnp.int32, (L, L), 1)
    causal = row >= col
    tri = jnp.where(causal, 1.0, 0.0).astype(F32)
    cum = _dot(tri, log_f, precision=HIGHEST)
    lane = lax.broadcasted_iota(jnp.int32, (L, LANES), 1)
    mat = jnp.where(lane < MLSTM_HEADS, gates, cum)
    mat_t = mat.T

    for h in range(MLSTM_HEADS):
        hs = slice(h * dh, (h + 1) * dh)
        q = (qk[:, h * dh:(h + 1) * dh] * (dh ** -0.5)).astype(BF16)
        k = qk[:, D_MLSTM + h * dh:D_MLSTM + (h + 1) * dh]
        v = v_ref[:, hs].astype(BF16)
        i_col = mat[:, h:h + 1]
        b_col = mat[:, MLSTM_HEADS + h:MLSTM_HEADS + h + 1]
        i_row = mat_t[h:h + 1, :]
        b_row = mat_t[MLSTM_HEADS + h:MLSTM_HEADS + h + 1, :]
        m_prev = m_ref[h][0:1, 0:1]
        c_prev = c_ref[h]
        n_prev = n_ref[h][0:1, :]

        log_d = jnp.where(causal, b_col - b_row + i_row, NEG_INF)
        a = b_col + m_prev
        m_comb = jnp.maximum(a, jnp.max(log_d, axis=-1, keepdims=True))
        w_intra = jnp.exp(log_d - m_comb)
        w_inter = jnp.exp(a - m_comb)
        s = _dot_nt(q, k.astype(BF16)) * w_intra
        num = _dot(s.astype(BF16), v) + w_inter * _dot(q, c_prev.astype(BF16))
        qf = q.astype(F32)
        den = (jnp.sum(s, axis=-1, keepdims=True)
               + w_inter * jnp.sum(qf * n_prev, axis=-1, keepdims=True))
        hout = num / jnp.maximum(jnp.abs(den), jnp.exp(-m_comb))

        b_last = b_col[L - 1:L, :]
        log_w = b_last - b_col + i_col
        m_new = jnp.maximum(b_last + m_prev, jnp.max(log_w, axis=0, keepdims=True))
        w_state = jnp.exp(log_w - m_new)
        decay = jnp.exp(b_last + m_prev - m_new)
        kw = w_state * k
        c_ref[h] = decay * c_prev + _dot_tn(kw.astype(BF16), v)
        n_ref[h] = jnp.broadcast_to(decay * n_prev + jnp.sum(kw, axis=0, keepdims=True),
                                    (SUBLANES, dh))
        m_ref[h] = jnp.broadcast_to(m_new, (SUBLANES, LANES))

        y = _rms(hout) * ng_ref[:, hs] * _sigmoid(o_ref[:, hs].astype(F32))
        out_ref[:, hs] = y.astype(BF16)


def _mlstm(proj, gates, conv_w, conv_b, norm_g, tl):
    t = proj.shape[0]
    L = CHUNK
    full = lambda *s: pl.BlockSpec(s, lambda i: (0,) * len(s))
    return pl.pallas_call(
        functools.partial(_mlstm_kernel, chunks=tl // L),
        grid=(t // tl,),
        in_specs=[pl.BlockSpec((tl, 2 * D_MLSTM), lambda i: (i, 1)),
                  pl.BlockSpec((tl, D_MLSTM), lambda i: (i, 4)),
                  pl.BlockSpec((tl, D_MLSTM), lambda i: (i, 5)),
                  pl.BlockSpec((tl, LANES), lambda i: (i, 0)),
                  full(CONV_WIDTH, 2 * D_MLSTM), full(1, 2 * D_MLSTM), full(1, D_MLSTM)],
        out_specs=pl.BlockSpec((tl, D_MLSTM), lambda i: (i, 0)),
        out_shape=jax.ShapeDtypeStruct((t, D_MLSTM), BF16),
        scratch_shapes=[pltpu.VMEM((L + SUBLANES, 2 * D_MLSTM), F32),
                        pltpu.VMEM((MLSTM_HEADS, MLSTM_HEAD_DIM, MLSTM_HEAD_DIM), F32),
                        pltpu.VMEM((MLSTM_HEADS, SUBLANES, MLSTM_HEAD_DIM), F32),
                        pltpu.VMEM((MLSTM_HEADS, SUBLANES, LANES), F32)],
        compiler_params=_params("arbitrary"),
        name="mlstm",
    )(proj, proj, proj, gates, conv_w, conv_b, norm_g)


def _outproj_kernel(yg_ref, ym_ref, x_ref, wo_ref, gm_ref, g_ref, sc_ref, sh_ref, wq_ref,
                    k1_ref, k2_ref, x1_ref, h2_ref, s1_ref, s2_ref):
    mix = _dot(yg_ref[...], wo_ref[0:D_GMLP, :]) + _dot(ym_ref[...], wo_ref[D_GMLP:, :])
    x1 = x_ref[...] + gm_ref[...] * mix
    x1_ref[...] = x1
    h2f = _rms(x1) * g_ref[...] * (1.0 + sc_ref[...]) + sh_ref[...]
    h2 = h2f.astype(BF16)
    h2_ref[...] = (h2f * math.sqrt(0.5)).astype(BF16)
    q = _dot(h2, wq_ref[...]).astype(BF16)
    k1 = k1_ref[...].astype(BF16)
    k2 = k2_ref[...].astype(BF16)
    for h in range(PEER_HEADS):
        base = h * 2 * PEER_HALF_DIM
        s1_ref[h] = _dot_nt(k1, q[:, base:base + PEER_HALF_DIM])
        s2_ref[h] = _dot_nt(k2, q[:, base + PEER_HALF_DIM:base + 2 * PEER_HALF_DIM])


def _outproj(yg, ym, x, w_out, gate_m, g, scale, shift, w_query, keys1, keys2, tm):
    t, d = x.shape
    row = lambda i: (0, 0)
    tok = lambda w: pl.BlockSpec((tm, w), lambda i: (i, 0))
    score = pl.BlockSpec((PEER_HEADS, PEER_N_KEYS, tm), lambda i: (0, 0, i))
    return pl.pallas_call(
        _outproj_kernel,
        grid=(t // tm,),
        in_specs=[tok(D_GMLP), tok(D_MLSTM), tok(d),
                  pl.BlockSpec((d, d), row, pipeline_mode=pl.Buffered(1)),
                  pl.BlockSpec((1, d), row), pl.BlockSpec((1, d), row),
                  pl.BlockSpec((1, d), row), pl.BlockSpec((1, d), row),
                  pl.BlockSpec((d, d), row, pipeline_mode=pl.Buffered(1)),
                  pl.BlockSpec((PEER_N_KEYS, PEER_HALF_DIM), row),
                  pl.BlockSpec((PEER_N_KEYS, PEER_HALF_DIM), row)],
        out_specs=[tok(d), tok(d), score, score],
        out_shape=[jax.ShapeDtypeStruct((t, d), F32), jax.ShapeDtypeStruct((t, d), BF16),
                   jax.ShapeDtypeStruct((PEER_HEADS, PEER_N_KEYS, t), F32),
                   jax.ShapeDtypeStruct((PEER_HEADS, PEER_N_KEYS, t), F32)],
        compiler_params=_params("parallel"),
        name="outproj",
    )(yg, ym, x, w_out, gate_m, g, scale, shift, w_query, keys1, keys2)


def _pair_candidates():
    return [(a, b) for a in range(PEER_TOPK) for b in range(PEER_TOPK)
            if (a + 1) * (b + 1) <= PEER_TOPK]


def _sort_network(n):
    def merge(lo, hi, r):
        step = r * 2
        if step < hi - lo:
            yield from merge(lo, hi, step)
            yield from merge(lo + r, hi, step)
            yield from [(i, i + r) for i in range(lo + r, hi - r, step)]
        else:
            yield (lo, lo + r)

    def sort(lo, hi):
        if hi - lo >= 1:
            mid = lo + (hi - lo) // 2
            yield from sort(lo, mid)
            yield from sort(mid + 1, hi)
            yield from merge(lo, hi, 1)

    return list(sort(0, n - 1))


def _compare_exchange(v, i, j):
    v[i], v[j] = jnp.maximum(v[i], v[j]), jnp.minimum(v[i], v[j])


def _bitonic_sort(v):
    d = PEER_TOPK // 2
    while d >= 1:
        for i in range(PEER_TOPK):
            if i & d == 0:
                _compare_exchange(v, i, i + d)
        d //= 2


def _top16(tiles):
    v = list(tiles)
    for i, j in _sort_network(PEER_TOPK):
        _compare_exchange(v, i, j)
    for shift in (4, 2, 1):
        other = [pltpu.roll(x, shift, axis=0) for x in v]
        v = [jnp.maximum(v[i], other[PEER_TOPK - 1 - i]) for i in range(PEER_TOPK)]
        _bitonic_sort(v)
    return v


def _route_kernel(s1_ref, s2_ref, thr_ref, e1_ref, e2_ref, *, cols):
    groups = PEER_N_KEYS // SUBLANES
    sub = lax.broadcasted_iota(jnp.int32, (SUBLANES, LANES), 0)
    pairs = _pair_candidates()

    def rows(g):
        return pl.ds(g * SUBLANES, SUBLANES)

    def column(c, carry):
        tok = pl.ds(pl.multiple_of(c * LANES, LANES), LANES)
        v1s = [jnp.zeros((SUBLANES, LANES), F32)] * PEER_TOPK
        v2s = [jnp.zeros((SUBLANES, LANES), F32)] * PEER_TOPK
        for h in range(PEER_HEADS):
            t1 = _top16([s1_ref[h, rows(g), tok] for g in range(groups)])
            t2 = _top16([s2_ref[h, rows(g), tok] for g in range(groups)])
            v1s = [jnp.where(sub == h, t1[a], v1s[a]) for a in range(PEER_TOPK)]
            v2s = [jnp.where(sub == h, t2[a], v2s[a]) for a in range(PEER_TOPK)]

        cands = [v1s[a] + v2s[b] for a, b in pairs]
        row = {ab: cnd for ab, cnd in zip(pairs, cands)}
        top = [row[(0, b)] for b in range(PEER_TOPK)]
        for a in range(1, PEER_TOPK):
            for b in range(PEER_TOPK // (a + 1)):
                i = PEER_TOPK - 1 - b
                top[i] = jnp.maximum(top[i], row[(a, b)])
            _bitonic_sort(top)
        tau = top[PEER_TOPK - 1]

        e2s = [jnp.exp(v2s[b] - v2s[0]) for b in range(PEER_TOPK)]
        thr_a = [jnp.full((SUBLANES, LANES), POS_INF, F32)] * PEER_TOPK
        zs_a = [jnp.zeros((SUBLANES, LANES), F32)] * PEER_TOPK
        for (a, b), cnd in zip(pairs, cands):
            hit = cnd >= tau
            thr_a[a] = jnp.where(hit, v2s[b], thr_a[a])
            zs_a[a] = zs_a[a] + jnp.where(hit, e2s[b], 0.0)
        z = jnp.zeros((SUBLANES, LANES), F32)
        for a in range(PEER_TOPK):
            z = z + jnp.exp(v1s[a] - v1s[0]) * zs_a[a]
        inv_z = math.sqrt(0.5) / z

        for h in range(PEER_HEADS):
            def rep(x):
                return jnp.broadcast_to(x[h:h + 1, :], (SUBLANES, LANES))
            v1r = [rep(v1s[a]) for a in range(PEER_TOPK)]
            thr_r = [rep(thr_a[a]) for a in range(PEER_TOPK)]
            max2 = rep(v2s[0])
            inv_zr = rep(inv_z)
            for g in range(groups):
                s1 = s1_ref[h, rows(g), tok]
                thr = jnp.full((SUBLANES, LANES), POS_INF, F32)
                for a in range(PEER_TOPK):
                    thr = jnp.where(s1 == v1r[a], thr_r[a], thr)
                thr_ref[h, rows(g), tok] = thr
                e1_ref[h, rows(g), tok] = jnp.exp(s1 - v1r[0]) * inv_zr
                e2_ref[h, rows(g), tok] = jnp.exp(s2_ref[h, rows(g), tok] - max2)
        return carry

    lax.fori_loop(0, cols, column, 0)


def _route(s1t, s2t, tc):
    t = s1t.shape[2]
    spec = pl.BlockSpec((PEER_HEADS, PEER_N_KEYS, tc), lambda i: (0, 0, i))
    shape = jax.ShapeDtypeStruct(s1t.shape, F32)
    return pl.pallas_call(
        functools.partial(_route_kernel, cols=tc // LANES),
        grid=(t // tc,),
        in_specs=[spec, spec],
        out_specs=[spec, spec, spec],
        out_shape=[shape, shape, shape],
        compiler_params=_params("parallel"),
        name="route",
    )(s1t, s2t)


def _peer_kernel(h2_ref, down_ref, up_ref, s2_ref, e2_ref, thr_ref, e1_ref, x1_ref, gf_ref, fg_ref,
                 out_ref, a0_ref, a1_ref, w0_ref, w1_ref, *, tb, nj, steps):
    n = pl.program_id(0)
    j_b = jnp.clip(n - 1, 0, steps - 1) % nj
    j_c = jnp.clip(n - 2, 0, steps - 1) % nj
    groups = PEER_N_KEYS // SUBLANES

    @pl.when(n == 0)
    def _():
        a1_ref[...] = jnp.zeros_like(a1_ref)
        w0_ref[...] = jnp.zeros_like(w0_ref)
        w1_ref[...] = jnp.zeros_like(w1_ref)

    @pl.when(j_c == 0)
    def _():
        out_ref[...] = jnp.zeros_like(out_ref)

    rows = pl.ds(pl.multiple_of(j_b * SUBLANES, SUBLANES), SUBLANES)

    def column(c, a_ref, w_ref):
        tok = pl.ds(c * LANES, LANES)
        thr8 = [thr_ref[h, rows, tok] for h in range(PEER_HEADS)]
        e18 = [e1_ref[h, rows, tok] for h in range(PEER_HEADS)]
        for s in range(SUBLANES):
            acc = [None] * groups
            for h in range(PEER_HEADS):
                thr = jnp.broadcast_to(thr8[h][s:s + 1, :], (SUBLANES, LANES))
                e1 = jnp.broadcast_to(e18[h][s:s + 1, :], (SUBLANES, LANES))
                for g in range(groups):
                    ks = pl.ds(g * SUBLANES, SUBLANES)
                    sel = jnp.where(s2_ref[h, ks, tok] >= thr, e2_ref[h, ks, tok], 0.0)
                    acc[g] = sel * e1 if h == 0 else acc[g] + sel * e1
            gt = jnp.concatenate(acc, axis=0)
            es = slice(s * PEER_N_KEYS, (s + 1) * PEER_N_KEYS)
            w_ref[tok, es] = (a_ref[tok, es].astype(F32) * gt.T).astype(BF16)

    def stages(a_write, a_read, w_write, w_read):
        av = _dot(h2_ref[...], down_ref[...])
        a_write[...] = (av + av * lax.erf(av)).astype(BF16)
        for c in range(tb // LANES):
            column(c, a_read, w_write)
        out_ref[...] += _dot(w_read[...], up_ref[...])

    @pl.when(n % 2 == 0)
    def _():
        stages(a0_ref, a1_ref, w1_ref, w0_ref)

    @pl.when(n % 2 == 1)
    def _():
        stages(a1_ref, a0_ref, w0_ref, w1_ref)

    @pl.when(j_c == nj - 1)
    def _():
        x2 = x1_ref[...] + gf_ref[...] * out_ref[...]
        out_ref[...] = _rms(x2) * fg_ref[...]


def _peer(h2, down_t, up_bf, s2t, e2t, thrt, e1t, x1, gate_f, final_g, tb):
    t, d = x1.shape
    se = SUBLANES * PEER_N_KEYS
    nj = PEER_N_EXPERTS // se
    steps = (t // tb) * nj
    stage_a = lambda n: jnp.minimum(n, steps - 1)
    stage_b = lambda n: jnp.clip(n - 1, 0, steps - 1)
    stage_c = lambda n: jnp.clip(n - 2, 0, steps - 1)
    row = lambda n: (0, 0)
    route = lambda: pl.BlockSpec((PEER_HEADS, PEER_N_KEYS, tb), lambda n: (0, 0, stage_b(n) // nj))
    return pl.pallas_call(
        functools.partial(_peer_kernel, tb=tb, nj=nj, steps=steps),
        grid=(steps + 2,),
        in_specs=[pl.BlockSpec((tb, d), lambda n: (stage_a(n) // nj, 0)),
                  pl.BlockSpec((d, se), lambda n: (0, stage_a(n) % nj)),
                  pl.BlockSpec((se, d), lambda n: (stage_c(n) % nj, 0)),
                  route(), route(), route(), route(),
                  pl.BlockSpec((tb, d), lambda n: (stage_c(n) // nj, 0)),
                  pl.BlockSpec((1, d), row), pl.BlockSpec((1, d), row)],
        out_specs=pl.BlockSpec((tb, d), lambda n: (stage_c(n) // nj, 0)),
        out_shape=jax.ShapeDtypeStruct((t, d), F32),
        scratch_shapes=[pltpu.VMEM((tb, se), BF16), pltpu.VMEM((tb, se), BF16),
                        pltpu.VMEM((tb, se), BF16), pltpu.VMEM((tb, se), BF16)],
        compiler_params=pltpu.CompilerParams(dimension_semantics=("arbitrary",),
                                             vmem_limit_bytes=PEER_VMEM_LIMIT),
        name="peer",
    )(h2, down_t, up_bf, s2t, e2t, thrt, e1t, x1, gate_f, final_g)


def _pick(n, *cands):
    for cnd in cands:
        if n % cnd == 0:
            return cnd
    raise ValueError(f"unsupported token count {n}")


def _tiles(t):
    return {
        "inproj": dict(tm=_pick(t, 1024, 512, 256, 128), tn=1536),
        "gmlp": dict(tg=_pick(t, 512, 256, 128)),
        "mlstm": dict(tl=_pick(t, 512, 256, 128)),
        "outproj": dict(tm=_pick(t, 256, 128)),
        "route": dict(tc=_pick(t, 512, 256, 128)),
        "peer": dict(tb=_pick(t, 512, 256, 128)),
    }


def kernel(x, c, ada_w, ada_b, norm_mix_g, w_in, b_gates, conv_w, conv_b, gmlp_norm_g, gmlp_w_spatial, gmlp_b_spatial, mlstm_norm_g, w_out, norm_ffn_g, peer_w_query, peer_sub_keys_1, peer_sub_keys_2, peer_expert_down, peer_expert_up, final_norm_g):
    bsz, seq, d = x.shape
    assert bsz == 1 and d == D_MODEL and seq % CHUNK == 0
    assert ada_w.shape[0] == 1
    t = bsz * seq
    xt = x.reshape(t, d)
    c8 = jnp.broadcast_to(c, (SUBLANES, d))
    pad = LANES - GATE_COLS
    tiles = _tiles(t)

    for l in range(1):
        mod = _ada(c8, ada_w[l], ada_b[l][None, :])[0:1]
        shift_m, scale_m, gate_m, shift_f, scale_f, gate_f = jnp.split(mod, N_MOD, axis=-1)

        w_bf = w_in[l].astype(BF16)
        w_gate = jnp.pad(w_bf[:, PROJ_COLS:], ((0, 0), (0, pad)))
        b_gate = jnp.pad(b_gates[l], (0, pad))[None, :]
        proj, gates = _inproj(xt, norm_mix_g[l][None, :], scale_m, shift_m, w_bf, w_gate, b_gate,
                              **tiles["inproj"])

        yg = _gmlp(proj, gmlp_w_spatial[l], gmlp_b_spatial[l].T, gmlp_norm_g[l], **tiles["gmlp"])
        ym = _mlstm(proj, gates, conv_w[l], conv_b[l][None, :], mlstm_norm_g[l].reshape(1, D_MLSTM),
                    **tiles["mlstm"])

        x1, h2, s1t, s2t = _outproj(yg, ym, xt, w_out[l].astype(BF16), gate_m,
                                    norm_ffn_g[l][None, :], scale_f, shift_f,
                                    peer_w_query[l].astype(BF16), peer_sub_keys_1[l],
                                    peer_sub_keys_2[l], **tiles["outproj"])
        thrt, e1t, e2t = _route(s1t, s2t, **tiles["route"])
        fg = final_norm_g[None, :]
        xt = _peer(h2, peer_expert_down[l].astype(BF16).T, peer_expert_up[l].astype(BF16),
                   s2t, e2t, thrt, e1t, x1, gate_f, fg, **tiles["peer"])
    return xt.reshape(bsz, seq, d)
```
